```python
import math
import jax, jax.numpy as jnp
from jax import lax
import numpy as np

D_MODEL = 1024
BATCH = 4
SEQ = 4096
DEPTH = 2

N_ATT_HEADS = 4
ATT_HEAD_DIM = 64
ATT_V_DIM = 2 * ATT_HEAD_DIM
ATT_QK_WIDTH = N_ATT_HEADS * 2 * ATT_HEAD_DIM
ATT_WIDTH = N_ATT_HEADS * ATT_V_DIM
ATT_SCALE = ATT_HEAD_DIM ** -0.5
Q_BLOCK = 128
SSM_WIDTH = D_MODEL - ATT_WIDTH
SSM_HEAD_DIM = 64
SSM_HEADS = SSM_WIDTH // SSM_HEAD_DIM
SSM_GROUPS = 2
SSM_STATE = 128
SSM_CONV = 4
SSM_CHUNK = 128
XBC_WIDTH = SSM_WIDTH + 2 * SSM_GROUPS * SSM_STATE
IN_SPLITS = (ATT_QK_WIDTH, ATT_QK_WIDTH, ATT_WIDTH, SSM_WIDTH, XBC_WIDTH, SSM_HEADS)
IN_WIDTH = sum(IN_SPLITS)
CONF_WIDTH = D_MODEL
CONF_WIDTH_K = 31
FFN_DIM = 2816
FFN_CONV = 3
RMS_EPS = 1e-6
LN_EPS = 1e-5

kernel_name = "hybrid_diffattn_ssd_conformer_convffn"


def rms_norm(x, w):
    xf = x.astype(jnp.float32)
    y = xf * lax.rsqrt(jnp.mean(xf * xf, axis=-1, keepdims=True) + RMS_EPS)
    return (y * w.astype(jnp.float32)).astype(x.dtype)


def layer_norm(x, w, b):
    xf = x.astype(jnp.float32)
    mu = jnp.mean(xf, axis=-1, keepdims=True)
    var = jnp.mean(jnp.square(xf - mu), axis=-1, keepdims=True)
    y = (xf - mu) * lax.rsqrt(var + LN_EPS)
    return (y * w.astype(jnp.float32) + b.astype(jnp.float32)).astype(x.dtype)


def causal_dwconv(x, w, b):
    k_w, c = w.shape
    xp = jnp.pad(x, ((0, 0), (k_w - 1, 0), (0, 0)))
    y = lax.conv_general_dilated(xp, w[:, None, :].astype(x.dtype), window_strides=(1,),
                                 padding='VALID', dimension_numbers=('NWC', 'WIO', 'NWC'),
                                 feature_group_count=c)
    return y + b.astype(x.dtype)


def segsum(x):
    t = x.shape[-1]
    xe = jnp.broadcast_to(x[..., :, None], x.shape + (t,))
    xe = jnp.where(jnp.tril(jnp.ones((t, t), bool), -1), xe, 0.0)
    cs = jnp.cumsum(xe, axis=-2)
    return jnp.where(jnp.tril(jnp.ones((t, t), bool), 0), cs, -jnp.inf)


def ssd_chunked(xdt, a_dt, bm, cm):
    b, l, h, p = xdt.shape
    g, n = bm.shape[2], bm.shape[3]
    r = h // g
    c = l // SSM_CHUNK
    t = SSM_CHUNK
    X = xdt.reshape(b, c, t, g, r, p)
    A = a_dt.reshape(b, c, t, g, r).transpose(0, 3, 4, 1, 2)
    Bc = bm.reshape(b, c, t, g, n)
    Cc = cm.reshape(b, c, t, g, n)
    A_cum = jnp.cumsum(A, axis=-1)
    Lmat = jnp.exp(segsum(A))
    CB = jnp.einsum('bclgn,bcsgn->bgcls', Cc, Bc)
    y_diag = jnp.einsum('bgcls,bgrcls,bcsgrp->bclgrp', CB, Lmat, X)
    decay_states = jnp.exp(A_cum[..., -1:] - A_cum)
    states = jnp.einsum('bclgn,bgrcl,bclgrp->bcgrpn', Bc, decay_states, X)
    chunk_a = jnp.pad(A_cum[..., -1], ((0, 0), (0, 0), (0, 0), (1, 0)))
    decay_chunk = jnp.exp(segsum(chunk_a))
    states_p = jnp.concatenate([jnp.zeros_like(states[:, :1]), states], axis=1)
    new_states = jnp.einsum('bgrzc,bcgrpn->bzgrpn', decay_chunk, states_p)[:, :-1]
    y_off = jnp.einsum('bclgn,bcgrpn,bgrcl->bclgrp', Cc, new_states, jnp.exp(A_cum))
    return (y_diag + y_off).reshape(b, l, h, p)


def diff_attention(q, k, v, lam):
    bsz, s = q.shape[0], q.shape[1]
    nb = s // Q_BLOCK
    qb = q.reshape(bsz, nb, Q_BLOCK, N_ATT_HEADS, 2, ATT_HEAD_DIM).transpose(1, 0, 2, 3, 4, 5)
    key_pos = jnp.arange(s)

    def one_block(args):
        q_blk, start = args
        sc = jnp.einsum('bqhcd,bkhcd->bhcqk', q_blk, k).astype(jnp.float32)
        causal = (start + jnp.arange(Q_BLOCK))[:, None] >= key_pos[None, :]
        sc = jnp.where(causal, sc, -jnp.inf)
        pr = jax.nn.softmax(sc, axis=-1)
        a = pr[:, :, 0] - lam * pr[:, :, 1]
        return jnp.einsum('bhqk,bkhv->bqhv', a.astype(v.dtype), v)

    o = lax.map(one_block, (qb, jnp.arange(nb) * Q_BLOCK))
    return o.transpose(1, 0, 2, 3, 4).reshape(bsz, s, N_ATT_HEADS, ATT_V_DIM)


def hybrid_mixer(h, lam_init, w_in, q_norm_w, k_norm_w, lam_q1, lam_k1, lam_q2, lam_k2,
                 attn_subln_w, ssm_conv_w, ssm_conv_b, ssm_dt_bias, ssm_A_log, ssm_D,
                 ssm_norm_w, w_out):
    bsz, s, _ = h.shape
    proj = h @ w_in.astype(h.dtype)
    idx = np.cumsum(IN_SPLITS)[:-1].tolist()
    q, k, v, z, xbc, dt_raw = jnp.split(proj, idx, axis=-1)
    q = rms_norm(q.reshape(bsz, s, N_ATT_HEADS, 2, ATT_HEAD_DIM), q_norm_w) * ATT_SCALE
    k = rms_norm(k.reshape(bsz, s, N_ATT_HEADS, 2, ATT_HEAD_DIM), k_norm_w)
    v = v.reshape(bsz, s, N_ATT_HEADS, ATT_V_DIM)
    f32 = jnp.float32
    lam = (jnp.exp(jnp.sum(lam_q1.astype(f32) * lam_k1.astype(f32)))
           - jnp.exp(jnp.sum(lam_q2.astype(f32) * lam_k2.astype(f32))) + lam_init)
    att = diff_attention(q, k, v, lam)
    att = (rms_norm(att, attn_subln_w) * (1.0 - lam_init)).reshape(bsz, s, ATT_WIDTH)
    xbc = jax.nn.silu(causal_dwconv(xbc, ssm_conv_w, ssm_conv_b))
    xs, bm, cm = jnp.split(xbc, [SSM_WIDTH, SSM_WIDTH + SSM_GROUPS * SSM_STATE], axis=-1)
    xs = xs.reshape(bsz, s, SSM_HEADS, SSM_HEAD_DIM)
    bm = bm.reshape(bsz, s, SSM_GROUPS, SSM_STATE)
    cm = cm.reshape(bsz, s, SSM_GROUPS, SSM_STATE)
    dt = jax.nn.softplus(dt_raw.astype(f32) + ssm_dt_bias.astype(f32))
    a = -jnp.exp(ssm_A_log.astype(f32))
    y = ssd_chunked(xs.astype(f32) * dt[..., None], dt * a, bm.astype(f32), cm.astype(f32))
    y = y + xs.astype(f32) * ssm_D.astype(f32)[:, None]
    y = y.reshape(bsz, s, SSM_WIDTH) * jax.nn.silu(z.astype(f32))
    y = rms_norm(y.reshape(bsz, s, SSM_GROUPS, SSM_WIDTH // SSM_GROUPS),
                 ssm_norm_w.reshape(SSM_GROUPS, -1)).reshape(bsz, s, SSM_WIDTH).astype(h.dtype)
    return jnp.concatenate([att, y], axis=-1) @ w_out.astype(h.dtype)


def conformer_conv(h, pw1_w, pw1_b, dw_w, dw_b, ln_w, ln_b, pw2_w, pw2_b):
    u = h @ pw1_w.astype(h.dtype) + pw1_b.astype(h.dtype)
    u = u[..., :CONF_WIDTH] * jax.nn.sigmoid(u[..., CONF_WIDTH:])
    u = causal_dwconv(u, dw_w, dw_b)
    u = jax.nn.silu(layer_norm(u, ln_w, ln_b))
    return u @ pw2_w.astype(h.dtype) + pw2_b.astype(h.dtype)


def conv_ffn(h, up_w, conv_w, conv_b, down_w):
    u = causal_dwconv(h @ up_w.astype(h.dtype), conv_w, conv_b)
    gate, val = u[..., :FFN_DIM], u[..., FFN_DIM:]
    return (jax.nn.silu(gate) * val) @ down_w.astype(h.dtype)


def setup_inputs(seed: int = 0) -> dict:
    key = jax.random.key(seed)
    ks = iter(jax.random.split(key, 48))
    ne, no = (DEPTH + 1) // 2, DEPTH // 2
    f32 = jnp.float32

    def nrm(shape, scale):
        return jax.random.normal(next(ks), shape, f32) * scale

    def gain(shape):
        return 1.0 + nrm(shape, 0.02)

    dt0 = jnp.exp(jax.random.uniform(next(ks), (ne, SSM_HEADS), f32)
                  * (math.log(0.1) - math.log(0.001)) + math.log(0.001))
    inputs = {
        "x": nrm((BATCH, SEQ, D_MODEL), 1.0),
        "mix_norm_w": gain((ne, D_MODEL)),
        "w_in": nrm((ne, D_MODEL, IN_WIDTH), D_MODEL ** -0.5),
        "q_norm_w": gain((ne, ATT_HEAD_DIM)),
        "k_norm_w": gain((ne, ATT_HEAD_DIM)),
        "lambda_q1": nrm((ne, ATT_HEAD_DIM), 0.1),
        "lambda_k1": nrm((ne, ATT_HEAD_DIM), 0.1),
        "lambda_q2": nrm((ne, ATT_HEAD_DIM), 0.1),
        "lambda_k2": nrm((ne, ATT_HEAD_DIM), 0.1),
        "attn_subln_w": gain((ne, ATT_V_DIM)),
        "ssm_conv_w": nrm((ne, SSM_CONV, XBC_WIDTH), SSM_CONV ** -0.5),
        "ssm_conv_b": nrm((ne, XBC_WIDTH), 0.02),
        "ssm_dt_bias": dt0 + jnp.log(-jnp.expm1(-dt0)),
        "ssm_A_log": jnp.log(jax.random.uniform(next(ks), (ne, SSM_HEADS), f32, 1.0, 16.0)),
        "ssm_D": gain((ne, SSM_HEADS)),
        "ssm_norm_w": gain((ne, SSM_WIDTH)),
        "w_out": nrm((ne, ATT_WIDTH + SSM_WIDTH, D_MODEL), (ATT_WIDTH + SSM_WIDTH) ** -0.5),
        "conf_norm_w": gain((no, D_MODEL)),
        "conf_pw1_w": nrm((no, D_MODEL, 2 * CONF_WIDTH), D_MODEL ** -0.5),
        "conf_pw1_b": nrm((no, 2 * CONF_WIDTH), 0.02),
        "conf_dw_w": nrm((no, CONF_WIDTH_K, CONF_WIDTH), CONF_WIDTH_K ** -0.5),
        "conf_dw_b": nrm((no, CONF_WIDTH), 0.02),
        "conf_ln_w": gain((no, CONF_WIDTH)),
        "conf_ln_b": nrm((no, CONF_WIDTH), 0.02),
        "conf_pw2_w": nrm((no, CONF_WIDTH, D_MODEL), CONF_WIDTH ** -0.5),
        "conf_pw2_b": nrm((no, D_MODEL), 0.02),
        "ffn_norm_w": gain((DEPTH, D_MODEL)),
        "ffn_up_w": nrm((DEPTH, D_MODEL, 2 * FFN_DIM), D_MODEL ** -0.5),
        "ffn_conv_w": nrm((DEPTH, FFN_CONV, 2 * FFN_DIM), FFN_CONV ** -0.5),
        "ffn_conv_b": nrm((DEPTH, 2 * FFN_DIM), 0.02),
        "ffn_down_w": nrm((DEPTH, FFN_DIM, D_MODEL), FFN_DIM ** -0.5),
    }
    return inputs


def reference(x, mix_norm_w, w_in, q_norm_w, k_norm_w, lambda_q1, lambda_k1, lambda_q2,
              lambda_k2, attn_subln_w, ssm_conv_w, ssm_conv_b, ssm_dt_bias, ssm_A_log, ssm_D,
              ssm_norm_w, w_out, conf_norm_w, conf_pw1_w, conf_pw1_b, conf_dw_w, conf_dw_b,
              conf_ln_w, conf_ln_b, conf_pw2_w, conf_pw2_b, ffn_norm_w, ffn_up_w, ffn_conv_w,
              ffn_conv_b, ffn_down_w):
    for i in range(DEPTH):
        if i % 2 == 0:
            e = i // 2
            lam_init = 0.8 - 0.6 * math.exp(-0.3 * i)
            x = x + hybrid_mixer(rms_norm(x, mix_norm_w[e]), lam_init, w_in[e], q_norm_w[e],
                                 k_norm_w[e], lambda_q1[e], lambda_k1[e], lambda_q2[e],
                                 lambda_k2[e], attn_subln_w[e], ssm_conv_w[e], ssm_conv_b[e],
                                 ssm_dt_bias[e], ssm_A_log[e], ssm_D[e], ssm_norm_w[e], w_out[e])
        else:
            o = i // 2
            x = x + conformer_conv(rms_norm(x, conf_norm_w[o]), conf_pw1_w[o], conf_pw1_b[o],
                                   conf_dw_w[o], conf_dw_b[o], conf_ln_w[o], conf_ln_b[o],
                                   conf_pw2_w[o], conf_pw2_b[o])
        x = x + conv_ffn(rms_norm(x, ffn_norm_w[i]), ffn_up_w[i], ffn_conv_w[i],
                         ffn_conv_b[i], ffn_down_w[i])
    return x
```

```python
import functools
import math

import jax
import jax.numpy as jnp
from jax import lax
from jax.experimental import pallas as pl
from jax.experimental.pallas import tpu as pltpu

F32 = jnp.float32
BF16 = jnp.bfloat16

D_MODEL = 1024
N_ATT_HEADS = 4
ATT_HEAD_DIM = 64
ATT_V_DIM = 2 * ATT_HEAD_DIM
ATT_WIDTH = N_ATT_HEADS * ATT_V_DIM
ATT_SCALE = ATT_HEAD_DIM ** -0.5
SSM_WIDTH = D_MODEL - ATT_WIDTH
SSM_HEAD_DIM = 64
SSM_HEADS = SSM_WIDTH // SSM_HEAD_DIM
SSM_GROUPS = 2
SSM_STATE = 128
SSM_CONV = 4
SSM_CHUNK = 128
XBC_WIDTH = SSM_WIDTH + 2 * SSM_GROUPS * SSM_STATE
CONF_WIDTH = D_MODEL
CONF_K = 31
FFN_DIM = 2816
FFN_CONV = 3
RMS_EPS = 1e-6
LN_EPS = 1e-5

LANES = 128
SUBLANES = 8
MXU_DIM = 256
VMEM_LIMIT = 56 * 1024 * 1024

ROW_TILE = 512
FFN_ROW_TILE = 256
ATT_TILE = 512
SSD_TILE = 512
CONV_ROWS = 64
CONF_HEAD = 32


def _dot(a, b):
    return jnp.dot(a, b, preferred_element_type=F32)


def _dot_nt(a, b):
    return lax.dot_general(a, b, (((1,), (1,)), ((), ())), preferred_element_type=F32)


def _dot_tn(a, b):
    return lax.dot_general(a, b, (((0,), (0,)), ((), ())), preferred_element_type=F32)


def _rms(x, w):
    ms = jnp.mean(x * x, axis=-1, keepdims=True)
    return x * lax.rsqrt(ms + RMS_EPS) * w


def _silu(x):
    return x * (1.0 / (1.0 + jnp.exp(-x)))


def _softplus(x):
    return jnp.maximum(x, 0.0) + jnp.log(1.0 + jnp.exp(-jnp.abs(x)))


def _split3(x):
    a = x.astype(BF16)
    r = x - a.astype(F32)
    b = r.astype(BF16)
    c = (r - b.astype(F32)).astype(BF16)
    return a, b, c


def _const_spec(shape):
    nd = len(shape)
    return pl.BlockSpec(shape, lambda *_: (0,) * nd)


def _params(sem):
    return pltpu.CompilerParams(dimension_semantics=sem, vmem_limit_bytes=VMEM_LIMIT)


def _inproj_kernel(x_ref, nw_ref, wq_ref, wk_ref, wv_ref, wz_ref, wx_ref, wdt_ref,
                   qnw_ref, knw_ref, grp_ref, cw_ref, cb_ref, dtb_ref,
                   q_ref, k_ref, v_ref, z_ref, xbc_ref, dt_ref, cbuf,
                   *, tm, tiles_per_seq):
    i = pl.program_id(0)
    h = _rms(x_ref[...], nw_ref[...]).astype(BF16)

    def qk_norm(w_ref, nw):
        p = _dot(h, w_ref[...])
        s_hi, s_lo, _ = _split3(p * p)
        ms = (_dot(s_hi, grp_ref[...]) + _dot(s_lo, grp_ref[...])) * (1.0 / ATT_HEAD_DIM)
        return p * lax.rsqrt(ms + RMS_EPS) * nw

    q_ref[...] = (qk_norm(wq_ref, qnw_ref[...]) * ATT_SCALE).astype(BF16)
    k_ref[...] = qk_norm(wk_ref, knw_ref[...]).astype(BF16)
    v_ref[...] = _dot(h, wv_ref[...]).astype(BF16)
    z_ref[...] = _silu(_dot(h, wz_ref[...])).astype(BF16)
    dt_ref[...] = _softplus(_dot(h, wdt_ref[...]) + dtb_ref[...])

    @pl.when(i % tiles_per_seq == 0)
    def _():
        cbuf[0:SUBLANES, :] = jnp.zeros((SUBLANES, XBC_WIDTH), F32)

    cbuf[SUBLANES:SUBLANES + tm, :] = _dot(h, wx_ref[...])
    base = SUBLANES - (SSM_CONV - 1)
    acc = cb_ref[...] + cw_ref[0:1, :] * cbuf[base:base + tm, :]
    for k in range(1, SSM_CONV):
        acc = acc + cw_ref[k:k + 1, :] * cbuf[base + k:base + k + tm, :]
    xbc_ref[...] = _silu(acc).astype(BF16)
    cbuf[0:SUBLANES, :] = cbuf[tm:tm + SUBLANES, :]


def _inproj(x2d, nw, wq, wk, wv, wz, wx, wdt, qnw, knw, grp, cw, cb, dtb, *, seq):
    m = x2d.shape[0]
    tm = ROW_TILE
    row = lambda c: pl.BlockSpec((tm, c), lambda i: (i, 0))
    ins = [x2d, nw, wq, wk, wv, wz, wx, wdt, qnw, knw, grp, cw, cb, dtb]
    in_specs = [row(D_MODEL)] + [_const_spec(a.shape) for a in ins[1:]]
    out_shape = [jax.ShapeDtypeStruct((m, ATT_WIDTH), BF16)] * 3 + [
        jax.ShapeDtypeStruct((m, SSM_WIDTH), BF16),
        jax.ShapeDtypeStruct((m, XBC_WIDTH), BF16),
        jax.ShapeDtypeStruct((m, LANES), F32)]
    out_specs = [row(ATT_WIDTH)] * 3 + [row(SSM_WIDTH), row(XBC_WIDTH), row(LANES)]
    return pl.pallas_call(
        functools.partial(_inproj_kernel, tm=tm, tiles_per_seq=seq // tm),
        grid=(m // tm,), in_specs=in_specs, out_specs=out_specs, out_shape=out_shape,
        scratch_shapes=[pltpu.VMEM((tm + SUBLANES, XBC_WIDTH), F32)],
        compiler_params=_params(("arbitrary",)), name="inproj")(*ins)


def _attn_kernel(q_ref, k_ref, v_ref, lq1_ref, lk1_ref, lq2_ref, lk2_ref, sw_ref, o_ref,
                 m1, l1, a1, m2, l2, a2, *, t, lam_init):
    i = pl.program_id(2)
    q = q_ref[0]
    lane = lax.broadcasted_iota(jnp.int32, q.shape, 1)
    zero = jnp.zeros_like(q)
    qa = jnp.where(lane < ATT_HEAD_DIM, q, zero)
    qb = jnp.where(lane >= ATT_HEAD_DIM, q, zero)

    for m_ref, l_ref, a_ref in ((m1, l1, a1), (m2, l2, a2)):
        m_ref[...] = jnp.full(m_ref.shape, -1e30, F32)
        l_ref[...] = jnp.zeros(l_ref.shape, F32)
        a_ref[...] = jnp.zeros(a_ref.shape, F32)

    def update(s, vb, m_ref, l_ref, a_ref):
        m_old = m_ref[...]
        m_new = jnp.maximum(m_old, jnp.max(s, axis=-1, keepdims=True))
        alpha = jnp.exp(m_old - m_new)
        p = jnp.exp(s - m_new)
        l_ref[...] = alpha * l_ref[...] + jnp.sum(p, axis=-1, keepdims=True)
        a_ref[...] = alpha * a_ref[...] + _dot(p.astype(BF16), vb)
        m_ref[...] = m_new

    def block(j, mask):
        r0 = pl.multiple_of(j * t, t)
        kb = k_ref[0, pl.ds(r0, t), :]
        vb = v_ref[0, pl.ds(r0, t), :]
        s1 = _dot_nt(qa, kb)
        s2 = _dot_nt(qb, kb)
        if mask is not None:
            s1 = jnp.where(mask, s1, -jnp.inf)
            s2 = jnp.where(mask, s2, -jnp.inf)
        update(s1, vb, m1, l1, a1)
        update(s2, vb, m2, l2, a2)

    def body(j, carry):
        block(j, None)
        return carry

    lax.fori_loop(0, i, body, 0)
    rr = lax.broadcasted_iota(jnp.int32, (t, t), 0)
    cc = lax.broadcasted_iota(jnp.int32, (t, t), 1)
    block(i, rr >= cc)

    lam = (jnp.exp(jnp.sum(lq1_ref[...] * lk1_ref[...], axis=-1, keepdims=True))
           - jnp.exp(jnp.sum(lq2_ref[...] * lk2_ref[...], axis=-1, keepdims=True)) + lam_init)
    o = a1[...] / l1[...] - lam * (a2[...] / l2[...])
    o_ref[0] = (_rms(o, sw_ref[...]) * (1.0 - lam_init)).astype(BF16)


def _attention(q, k, v, lq1, lk1, lq2, lk2, sw, *, lam_init):
    b, s, _ = q.shape
    t = ATT_TILE
    qspec = pl.BlockSpec((1, t, ATT_V_DIM), lambda bi, hi, i: (bi, i, hi))
    kvspec = pl.BlockSpec((1, s, ATT_V_DIM), lambda bi, hi, i: (bi, 0, hi))
    small = [lq1, lk1, lq2, lk2, sw]
    stat = pltpu.VMEM((t, 1), F32)
    acc = pltpu.VMEM((t, ATT_V_DIM), F32)
    return pl.pallas_call(
        functools.partial(_attn_kernel, t=t, lam_init=lam_init),
        grid=(b, N_ATT_HEADS, s // t),
        in_specs=[qspec, kvspec, kvspec] + [_const_spec(a.shape) for a in small],
        out_specs=qspec, out_shape=jax.ShapeDtypeStruct((b, s, ATT_WIDTH), BF16),
        scratch_shapes=[stat, stat, acc, stat, stat, acc],
        compiler_params=_params(("arbitrary", "arbitrary", "arbitrary")),
        name="diff_attn")(q, k, v, *small)


def _expand_heads(v):
    r = v.shape[0]
    lane = lax.broadcasted_iota(jnp.int32, (r, LANES), 1)
    blocks = []
    for b in range(SSM_WIDTH // LANES):
        lo = jnp.broadcast_to(v[:, 2 * b:2 * b + 1], (r, LANES))
        hi = jnp.broadcast_to(v[:, 2 * b + 1:2 * b + 2], (r, LANES))
        blocks.append(jnp.where(lane < SSM_HEAD_DIM, lo, hi))
    return jnp.concatenate(blocks, axis=1)


def _ssd_kernel(xs_ref, b_ref, c_ref, dt_ref, z_ref, alog_ref, dexp_ref, nw_ref, y_ref, state,
                *, tc, tiles_per_seq):
    i = pl.program_id(0)
    t = SSM_CHUNK
    gw = SSM_WIDTH // SSM_GROUPS
    hpg = SSM_HEADS // SSM_GROUPS

    @pl.when(i % tiles_per_seq == 0)
    def _():
        state[...] = jnp.zeros(state.shape, F32)

    a_head = -jnp.exp(alog_ref[...])
    rr = lax.broadcasted_iota(jnp.int32, (t, t), 0)
    cc = lax.broadcasted_iota(jnp.int32, (t, t), 1)
    tril = rr >= cc
    tri = tril.astype(BF16)
    lo_half = cc < SSM_HEAD_DIM

    def chunk(ci, carry):
        r0 = pl.multiple_of(ci * t, t)
        rows = pl.ds(r0, t)
        xs = xs_ref[rows, :].astype(F32)
        dt = dt_ref[rows, :]
        a = dt * a_head
        p0, p1, p2 = _split3(a)
        acum = _dot(tri, p0) + _dot(tri, p1) + _dot(tri, p2)
        acum_t = acum.T
        alast = acum[t - 1:t, :]
        dtx = _expand_heads(dt)
        ea = _expand_heads(jnp.exp(acum))
        dec = _expand_heads(jnp.exp(alast - acum))
        elast = _expand_heads(jnp.exp(alast))
        x = xs * dtx
        xb = x.astype(BF16)
        xdec = (x * dec).astype(BF16)
        ys = []
        for g in range(SSM_GROUPS):
            gs = slice(g * gw, (g + 1) * gw)
            bg = b_ref[rows, g * SSM_STATE:(g + 1) * SSM_STATE]
            cg = c_ref[rows, g * SSM_STATE:(g + 1) * SSM_STATE]
            cb = _dot_nt(cg, bg)
            st = state[g]
            y_g = _dot(cg, st.astype(BF16)) * ea[:, gs]
            pairs = []
            for pr in range(hpg // 2):
                xp = xb[:, g * gw + pr * LANES:g * gw + (pr + 1) * LANES]
                res = []
                for hh in range(2):
                    hd = g * hpg + 2 * pr + hh
                    diff = acum[:, hd:hd + 1] - acum_t[hd:hd + 1, :]
                    lm = jnp.exp(jnp.where(tril, diff, -jnp.inf))
                    res.append(_dot((cb * lm).astype(BF16), xp))
                pairs.append(jnp.where(lo_half, res[0], res[1]))
            ys.append(y_g + jnp.concatenate(pairs, axis=1))
            state[g] = st * elast[:, gs] + _dot_tn(bg, xdec[:, gs])
        y = jnp.concatenate(ys, axis=1) + xs * dexp_ref[...]
        y = y * z_ref[rows, :].astype(F32)
        outs = []
        for g in range(SSM_GROUPS):
            gs = slice(g * gw, (g + 1) * gw)
            outs.append(_rms(y[:, gs], nw_ref[:, gs]))
        y_ref[rows, :] = jnp.concatenate(outs, axis=1).astype(BF16)
        return carry

    lax.fori_loop(0, tc // t, chunk, 0)


def _ssd(xbc, dt, zs, alog, dexp, nw, *, seq):
    m = xbc.shape[0]
    tc = SSD_TILE
    bc_w = SSM_GROUPS * SSM_STATE
    ins = [xbc, xbc, xbc, dt, zs, alog, dexp, nw]
    in_specs = [pl.BlockSpec((tc, SSM_WIDTH), lambda i: (i, 0)),
                pl.BlockSpec((tc, bc_w), lambda i: (i, SSM_WIDTH // bc_w)),
                pl.BlockSpec((tc, bc_w), lambda i: (i, SSM_WIDTH // bc_w + 1)),
                pl.BlockSpec((tc, LANES), lambda i: (i, 0)),
                pl.BlockSpec((tc, SSM_WIDTH), lambda i: (i, 0))] + [
                    _const_spec(a.shape) for a in ins[5:]]
    return pl.pallas_call(
        functools.partial(_ssd_kernel, tc=tc, tiles_per_seq=seq // tc),
        grid=(m // tc,), in_specs=in_specs,
        out_specs=pl.BlockSpec((tc, SSM_WIDTH), lambda i: (i, 0)),
        out_shape=jax.ShapeDtypeStruct((m, SSM_WIDTH), BF16),
        scratch_shapes=[pltpu.VMEM((SSM_GROUPS, SSM_STATE, SSM_WIDTH // SSM_GROUPS), F32)],
        compiler_params=_params(("arbitrary",)), name="ssd")(*ins)


def _outproj_kernel(x_ref, att_ref, y_ref, wa_ref, wy_ref, o_ref):
    o_ref[...] = x_ref[...] + _dot(att_ref[...], wa_ref[...]) + _dot(y_ref[...], wy_ref[...])


def _outproj(x2d, att, y, wa, wy):
    m = x2d.shape[0]
    tm = ROW_TILE
    row = lambda c: pl.BlockSpec((tm, c), lambda i: (i, 0))
    return pl.pallas_call(
        _outproj_kernel, grid=(m // tm,),
        in_specs=[row(D_MODEL), row(ATT_WIDTH), row(SSM_WIDTH),
                  _const_spec(wa.shape), _const_spec(wy.shape)],
        out_specs=row(D_MODEL), out_shape=jax.ShapeDtypeStruct((m, D_MODEL), F32),
        compiler_params=_params(("arbitrary",)), name="outproj")(x2d, att, y, wa, wy)


def _ffn_kernel(x_ref, nw_ref, upg_ref, upv_ref, cwg_ref, cbg_ref, cwv_ref, cbv_ref, down_ref,
                o_ref, gbuf, vbuf, act, *, tm, tiles_per_seq, fc):
    i = pl.program_id(0)
    x = x_ref[...]
    h = _rms(x, nw_ref[...]).astype(BF16)
    nc = FFN_DIM // fc

    @pl.when(i % tiles_per_seq == 0)
    def _():
        gbuf[:, 0:SUBLANES, :] = jnp.zeros((nc, SUBLANES, fc), F32)
        vbuf[:, 0:SUBLANES, :] = jnp.zeros((nc, SUBLANES, fc), F32)

    base = SUBLANES - (FFN_CONV - 1)

    def conv(buf, c, cw_ref, cb_ref, sl):
        acc = cb_ref[:, sl] + cw_ref[0:1, sl] * buf[c, base:base + tm, :]
        for k in range(1, FFN_CONV):
            acc = acc + cw_ref[k:k + 1, sl] * buf[c, base + k:base + k + tm, :]
        return acc

    for c in range(nc):
        sl = slice(c * fc, (c + 1) * fc)
        gbuf[c, SUBLANES:SUBLANES + tm, :] = _dot(h, upg_ref[:, sl])
        vbuf[c, SUBLANES:SUBLANES + tm, :] = _dot(h, upv_ref[:, sl])
        g = conv(gbuf, c, cwg_ref, cbg_ref, sl)
        v = conv(vbuf, c, cwv_ref, cbv_ref, sl)
        act[:, sl] = (_silu(g) * v).astype(BF16)
        gbuf[c, 0:SUBLANES, :] = gbuf[c, tm:tm + SUBLANES, :]
        vbuf[c, 0:SUBLANES, :] = vbuf[c, tm:tm + SUBLANES, :]

    o_ref[...] = x + _dot(act[...], down_ref[...])


def _ffn(x2d, nw, upg, upv, cwg, cbg, cwv, cbv, down, *, seq):
    m = x2d.shape[0]
    tm = FFN_ROW_TILE
    fc = MXU_DIM
    nc = FFN_DIM // fc
    row = pl.BlockSpec((tm, D_MODEL), lambda i: (i, 0))
    ins = [x2d, nw, upg, upv, cwg, cbg, cwv, cbv, down]
    return pl.pallas_call(
        functools.partial(_ffn_kernel, tm=tm, tiles_per_seq=seq // tm, fc=fc),
        grid=(m // tm,), in_specs=[row] + [_const_spec(a.shape) for a in ins[1:]],
        out_specs=row, out_shape=jax.ShapeDtypeStruct((m, D_MODEL), F32),
        scratch_shapes=[pltpu.VMEM((nc, tm + SUBLANES, fc), F32),
                        pltpu.VMEM((nc, tm + SUBLANES, fc), F32),
                        pltpu.VMEM((tm, FFN_DIM), BF16)],
        compiler_params=_params(("arbitrary",)), name="conv_ffn")(*ins)


def _conformer_kernel(x_ref, nw_ref, w1a_ref, w1b_ref, b1a_ref, b1b_ref, dww_ref, dwb_ref,
                      lnw_ref, lnb_ref, w2_ref, b2_ref, o_ref, cbuf, ybuf, *, tm, tiles_per_seq):
    i = pl.program_id(0)
    x = x_ref[...]
    h = _rms(x, nw_ref[...]).astype(BF16)

    @pl.when(i % tiles_per_seq == 0)
    def _():
        cbuf[0:CONF_HEAD, :] = jnp.zeros((CONF_HEAD, CONF_WIDTH), F32)

    ua = _dot(h, w1a_ref[...]) + b1a_ref[...]
    ub = _dot(h, w1b_ref[...]) + b1b_ref[...]
    cbuf[CONF_HEAD:CONF_HEAD + tm, :] = ua * (1.0 / (1.0 + jnp.exp(-ub)))

    base = CONF_HEAD - (CONF_K - 1)
    for rb in range(tm // CONV_ROWS):
        r0 = rb * CONV_ROWS
        for cblk in range(CONF_WIDTH // LANES):
            cs = slice(cblk * LANES, (cblk + 1) * LANES)
            acc = dwb_ref[:, cs] + dww_ref[0:1, cs] * cbuf[r0 + base:r0 + base + CONV_ROWS, cs]
            for k in range(1, CONF_K):
                acc = acc + dww_ref[k:k + 1, cs] * cbuf[r0 + base + k:r0 + base + k + CONV_ROWS, cs]
            ybuf[r0:r0 + CONV_ROWS, cs] = acc
    cbuf[0:CONF_HEAD, :] = cbuf[tm:tm + CONF_HEAD, :]

    u = ybuf[...]
    mu = jnp.mean(u, axis=-1, keepdims=True)
    d = u - mu
    var = jnp.mean(d * d, axis=-1, keepdims=True)
    y = _silu(d * lax.rsqrt(var + LN_EPS) * lnw_ref[...] + lnb_ref[...]).astype(BF16)
    o_ref[...] = x + _dot(y, w2_ref[...]) + b2_ref[...]


def _conformer(x2d, nw, w1a, w1b, b1a, b1b, dww, dwb, lnw, lnb, w2, b2, *, seq):
    m = x2d.shape[0]
    tm = FFN_ROW_TILE
    row = pl.BlockSpec((tm, D_MODEL), lambda i: (i, 0))
    ins = [x2d, nw, w1a, w1b, b1a, b1b, dww, dwb, lnw, lnb, w2, b2]
    return pl.pallas_call(
        functools.partial(_conformer_kernel, tm=tm, tiles_per_seq=seq // tm),
        grid=(m // tm,), in_specs=[row] + [_const_spec(a.shape) for a in ins[1:]],
        out_specs=row, out_shape=jax.ShapeDtypeStruct((m, D_MODEL), F32),
        scratch_shapes=[pltpu.VMEM((tm + CONF_HEAD, CONF_WIDTH), F32),
                        pltpu.VMEM((tm, CONF_WIDTH), F32)],
        compiler_params=_params(("arbitrary",)), name="conformer")(*ins)


def _row(v):
    return v.reshape(1, -1).astype(F32)


def _pad_lanes(v, width=LANES):
    return jnp.pad(v, ((0, 0), (0, width - v.shape[1])))


def _mixer_layer(x2d, seq, lam_init, mix_norm_w, w_in, q_norm_w, k_norm_w, lq1, lk1, lq2, lk2,
                 attn_subln_w, conv_w, conv_b, dt_bias, a_log, d_skip, ssm_norm_w, w_out):
    m = x2d.shape[0]
    bsz = m // seq
    o = 0
    cols = []
    for width in (ATT_WIDTH, ATT_WIDTH, ATT_WIDTH, SSM_WIDTH, XBC_WIDTH, SSM_HEADS):
        cols.append(w_in[:, o:o + width].astype(BF16))
        o += width
    wq, wk, wv, wz, wx, wdt = cols
    wdt = _pad_lanes(wdt)
    reps = ATT_WIDTH // ATT_HEAD_DIM
    grp_id = jnp.arange(ATT_WIDTH) // ATT_HEAD_DIM
    grp = (grp_id[:, None] == grp_id[None, :]).astype(BF16)
    q, k, v, zs, xbc, dt = _inproj(
        x2d, _row(mix_norm_w), wq, wk, wv, wz, wx, wdt,
        _row(jnp.tile(q_norm_w, reps)), _row(jnp.tile(k_norm_w, reps)), grp,
        conv_w.astype(F32), _row(conv_b), _pad_lanes(_row(dt_bias)), seq=seq)
    shp = (bsz, seq, ATT_WIDTH)
    att = _attention(q.reshape(shp), k.reshape(shp), v.reshape(shp),
                     _row(lq1), _row(lk1), _row(lq2), _row(lk2), _row(attn_subln_w),
                     lam_init=lam_init).reshape(m, ATT_WIDTH)
    y = _ssd(xbc, dt, zs, _pad_lanes(_row(a_log)), _row(jnp.repeat(d_skip, SSM_HEAD_DIM)),
             _row(ssm_norm_w), seq=seq)
    w_out = w_out.astype(BF16)
    return _outproj(x2d, att, y, w_out[:ATT_WIDTH], w_out[ATT_WIDTH:])


def _ffn_layer(x2d, seq, norm_w, up_w, conv_w, conv_b, down_w):
    up = up_w.astype(BF16)
    cw = conv_w.astype(F32)
    cb = _row(conv_b)
    return _ffn(x2d, _row(norm_w), up[:, :FFN_DIM], up[:, FFN_DIM:],
                cw[:, :FFN_DIM], cb[:, :FFN_DIM], cw[:, FFN_DIM:], cb[:, FFN_DIM:],
                down_w.astype(BF16), seq=seq)


def _conformer_layer(x2d, seq, norm_w, pw1_w, pw1_b, dw_w, dw_b, ln_w, ln_b, pw2_w, pw2_b):
    w1 = pw1_w.astype(BF16)
    b1 = _row(pw1_b)
    return _conformer(x2d, _row(norm_w), w1[:, :CONF_WIDTH], w1[:, CONF_WIDTH:],
                      b1[:, :CONF_WIDTH], b1[:, CONF_WIDTH:], dw_w.astype(F32), _row(dw_b),
                      _row(ln_w), _row(ln_b), pw2_w.astype(BF16), _row(pw2_b), seq=seq)


def kernel(x, mix_norm_w, w_in, q_norm_w, k_norm_w, lambda_q1, lambda_k1, lambda_q2, lambda_k2,
           attn_subln_w, ssm_conv_w, ssm_conv_b, ssm_dt_bias, ssm_A_log, ssm_D, ssm_norm_w, w_out,
           conf_norm_w, conf_pw1_w, conf_pw1_b, conf_dw_w, conf_dw_b, conf_ln_w, conf_ln_b,
           conf_pw2_w, conf_pw2_b, ffn_norm_w, ffn_up_w, ffn_conv_w, ffn_conv_b, ffn_down_w):
    bsz, seq, d = x.shape
    depth = ffn_norm_w.shape[0]
    h = x.reshape(bsz * seq, d)
    for i in range(depth):
        if i % 2 == 0:
            e = i // 2
            lam_init = 0.8 - 0.6 * math.exp(-0.3 * i)
            h = _mixer_layer(h, seq, lam_init, mix_norm_w[e], w_in[e], q_norm_w[e], k_norm_w[e],
                             lambda_q1[e], lambda_k1[e], lambda_q2[e], lambda_k2[e],
                             attn_subln_w[e], ssm_conv_w[e], ssm_conv_b[e], ssm_dt_bias[e],
                             ssm_A_log[e], ssm_D[e], ssm_norm_w[e], w_out[e])
        else:
            o = i // 2
            h = _conformer_layer(h, seq, conf_norm_w[o], conf_pw1_w[o], conf_pw1_b[o],
                                 conf_dw_w[o], conf_dw_b[o], conf_ln_w[o], conf_ln_b[o],
                                 conf_pw2_w[o], conf_pw2_b[o])
        h = _ffn_layer(h, seq, ffn_norm_w[i], ffn_up_w[i], ffn_conv_w[i], ffn_conv_b[i],
                       ffn_down_w[i])
    return h.reshape(bsz, seq, d)
```

```python
import functools
import math

import jax
import jax.numpy as jnp
from jax import lax
from jax.experimental import pallas as pl
from jax.experimental.pallas import tpu as pltpu

F32 = jnp.float32
BF16 = jnp.bfloat16

D_MODEL = 1024
N_ATT_HEADS = 4
ATT_HEAD_DIM = 64
ATT_V_DIM = 2 * ATT_HEAD_DIM
ATT_WIDTH = N_ATT_HEADS * ATT_V_DIM
ATT_SCALE = ATT_HEAD_DIM ** -0.5
SSM_WIDTH = D_MODEL - ATT_WIDTH
SSM_HEAD_DIM = 64
SSM_HEADS = SSM_WIDTH // SSM_HEAD_DIM
SSM_GROUPS = 2
SSM_STATE = 128
SSM_CONV = 4
SSM_CHUNK = 128
XBC_WIDTH = SSM_WIDTH + 2 * SSM_GROUPS * SSM_STATE
CONF_WIDTH = D_MODEL
CONF_K = 31
FFN_DIM = 2816
FFN_CONV = 3
RMS_EPS = 1e-6
LN_EPS = 1e-5

LANES = 128
SUBLANES = 8
MXU_DIM = 256
VMEM_LIMIT = 56 * 1024 * 1024

ROW_TILE = 512
FFN_ROW_TILE = 256
ATT_TILE = 512
SSD_TILE = 512
CONV_STRIDE = 4
CONV_BLOCK = SUBLANES * CONV_STRIDE
CONF_HEAD = 32


def _dot(a, b):
    return jnp.dot(a, b, preferred_element_type=F32)


def _dot_nt(a, b):
    return lax.dot_general(a, b, (((1,), (1,)), ((), ())), preferred_element_type=F32)


def _dot_tn(a, b):
    return lax.dot_general(a, b, (((0,), (0,)), ((), ())), preferred_element_type=F32)


def _rms(x, w):
    ms = jnp.mean(x * x, axis=-1, keepdims=True)
    return x * lax.rsqrt(ms + RMS_EPS) * w


def _sigmoid(x):
    return 1.0 / (1.0 + jnp.exp(-x))


def _silu(x):
    return x * _sigmoid(x)


def _softplus(x):
    return jnp.maximum(x, 0.0) + jnp.log(1.0 + jnp.exp(-jnp.abs(x)))


def _split3(x):
    a = x.astype(BF16)
    r = x - a.astype(F32)
    b = r.astype(BF16)
    c = (r - b.astype(F32)).astype(BF16)
    return a, b, c


def _const_spec(shape):
    nd = len(shape)
    return pl.BlockSpec(shape, lambda *_: (0,) * nd, pipeline_mode=pl.Buffered(1))


def _params(sem):
    return pltpu.CompilerParams(dimension_semantics=sem, vmem_limit_bytes=VMEM_LIMIT)


def _lane_chunk(c):
    return slice(c * LANES, (c + 1) * LANES)


def _conv_rows(buf, c, w_ref, b_ref, row, head, ntaps):
    cs = _lane_chunk(c)
    first = head - (ntaps - 1) + row
    acc = b_ref[:, cs] + w_ref[0:1, cs] * buf[c, pl.ds(first, SUBLANES, stride=CONV_STRIDE), :]
    for k in range(1, ntaps):
        acc = acc + w_ref[k:k + 1, cs] * buf[c, pl.ds(first + k, SUBLANES, stride=CONV_STRIDE), :]
    return acc


def _strided_rows(tm):
    return [blk * CONV_BLOCK + b for blk in range(tm // CONV_BLOCK) for b in range(CONV_STRIDE)]


def _inproj_kernel(x_ref, nw_ref, wq_ref, wk_ref, wv_ref, wz_ref, wx_ref, wdt_ref,
                   qnw_ref, knw_ref, grp_ref, cw_ref, cb_ref, dtb_ref,
                   q_ref, k_ref, v_ref, z_ref, xbc_ref, dt_ref, cbuf, ybuf,
                   *, tm, tiles_per_seq):
    i = pl.program_id(0)
    h = _rms(x_ref[...], nw_ref[...]).astype(BF16)
    nchunk = XBC_WIDTH // LANES

    def qk_norm(w_ref, nw):
        p = _dot(h, w_ref[...])
        s_hi, s_lo, _ = _split3(p * p)
        ms = (_dot(s_hi, grp_ref[...]) + _dot(s_lo, grp_ref[...])) * (1.0 / ATT_HEAD_DIM)
        return p * lax.rsqrt(ms + RMS_EPS) * nw

    q_ref[...] = (qk_norm(wq_ref, qnw_ref[...]) * ATT_SCALE).astype(BF16)
    k_ref[...] = qk_norm(wk_ref, knw_ref[...]).astype(BF16)
    v_ref[...] = _dot(h, wv_ref[...]).astype(BF16)
    z_ref[...] = _silu(_dot(h, wz_ref[...])).astype(BF16)
    dt_ref[...] = _softplus(_dot(h, wdt_ref[...]) + dtb_ref[...])

    @pl.when(i % tiles_per_seq == 0)
    def _():
        cbuf[:, 0:SUBLANES, :] = jnp.zeros((nchunk, SUBLANES, LANES), F32)

    raw = _dot(h, wx_ref[...])
    for c in range(nchunk):
        cbuf[c, SUBLANES:SUBLANES + tm, :] = raw[:, _lane_chunk(c)]
    for c in range(nchunk):
        for row in _strided_rows(tm):
            acc = _conv_rows(cbuf, c, cw_ref, cb_ref, row, SUBLANES, SSM_CONV)
            ybuf[c, pl.ds(row, SUBLANES, stride=CONV_STRIDE), :] = _silu(acc)
        xbc_ref[:, _lane_chunk(c)] = ybuf[c].astype(BF16)
        cbuf[c, 0:SUBLANES, :] = cbuf[c, tm:tm + SUBLANES, :]


def _inproj(x2d, nw, wq, wk, wv, wz, wx, wdt, qnw, knw, grp, cw, cb, dtb, *, seq):
    m = x2d.shape[0]
    tm = ROW_TILE
    row = lambda c: pl.BlockSpec((tm, c), lambda i: (i, 0))
    ins = [x2d, nw, wq, wk, wv, wz, wx, wdt, qnw, knw, grp, cw, cb, dtb]
    in_specs = [row(D_MODEL)] + [_const_spec(a.shape) for a in ins[1:]]
    out_shape = [jax.ShapeDtypeStruct((m, ATT_WIDTH), BF16)] * 3 + [
        jax.ShapeDtypeStruct((m, SSM_WIDTH), BF16),
        jax.ShapeDtypeStruct((m, XBC_WIDTH), BF16),
        jax.ShapeDtypeStruct((m, LANES), F32)]
    out_specs = [row(ATT_WIDTH)] * 3 + [row(SSM_WIDTH), row(XBC_WIDTH), row(LANES)]
    nchunk = XBC_WIDTH // LANES
    return pl.pallas_call(
        functools.partial(_inproj_kernel, tm=tm, tiles_per_seq=seq // tm),
        grid=(m // tm,), in_specs=in_specs, out_specs=out_specs, out_shape=out_shape,
        scratch_shapes=[pltpu.VMEM((nchunk, tm + SUBLANES, LANES), F32),
                        pltpu.VMEM((nchunk, tm, LANES), F32)],
        compiler_params=_params(("arbitrary",)), name="inproj")(*ins)


def _attn_kernel(q_ref, k_ref, v_ref, lq1_ref, lk1_ref, lq2_ref, lk2_ref, sw_ref, o_ref,
                 vext, m1, a1, m2, a2, *, t, lam_init):
    i = pl.program_id(2)

    @pl.when(i == 0)
    def _():
        vext[:, 0:ATT_V_DIM] = v_ref[0]
        vext[:, ATT_V_DIM:] = jnp.ones((vext.shape[0], ATT_V_DIM), BF16)

    q = q_ref[0]
    lane = lax.broadcasted_iota(jnp.int32, q.shape, 1)
    zero = jnp.zeros_like(q)
    qa = jnp.where(lane < ATT_HEAD_DIM, q, zero)
    qb = jnp.where(lane >= ATT_HEAD_DIM, q, zero)

    for m_ref, a_ref in ((m1, a1), (m2, a2)):
        m_ref[...] = jnp.full(m_ref.shape, -1e30, F32)
        a_ref[...] = jnp.zeros(a_ref.shape, F32)

    def update(s, vb, m_ref, a_ref):
        m_old = m_ref[...]
        m_new = jnp.maximum(m_old, jnp.max(s, axis=-1, keepdims=True))
        alpha = jnp.exp(m_old - m_new)
        p = jnp.exp(s - pltpu.repeat(m_new, t // LANES, axis=1))
        a_ref[...] = pltpu.repeat(alpha, 2, axis=1) * a_ref[...] + _dot(p.astype(BF16), vb)
        m_ref[...] = m_new

    def block(j, mask):
        r0 = pl.multiple_of(j * t, t)
        kb = k_ref[0, pl.ds(r0, t), :]
        vb = vext[pl.ds(r0, t), :]
        s1 = _dot_nt(qa, kb)
        s2 = _dot_nt(qb, kb)
        if mask is not None:
            s1 = jnp.where(mask, s1, -jnp.inf)
            s2 = jnp.where(mask, s2, -jnp.inf)
        update(s1, vb, m1, a1)
        update(s2, vb, m2, a2)

    def body(j, carry):
        block(j, None)
        return carry

    lax.fori_loop(0, i, body, 0)
    rr = lax.broadcasted_iota(jnp.int32, (t, t), 0)
    cc = lax.broadcasted_iota(jnp.int32, (t, t), 1)
    block(i, rr >= cc)

    lam = (jnp.exp(jnp.sum(lq1_ref[...] * lk1_ref[...], axis=-1, keepdims=True))
           - jnp.exp(jnp.sum(lq2_ref[...] * lk2_ref[...], axis=-1, keepdims=True)) + lam_init)
    o = (a1[:, 0:ATT_V_DIM] / a1[:, ATT_V_DIM:]
         - lam * (a2[:, 0:ATT_V_DIM] / a2[:, ATT_V_DIM:]))
    o_ref[0] = (_rms(o, sw_ref[...]) * (1.0 - lam_init)).astype(BF16)


def _attention(q, k, v, lq1, lk1, lq2, lk2, sw, *, lam_init):
    b, s, _ = q.shape
    t = ATT_TILE
    qspec = pl.BlockSpec((1, t, ATT_V_DIM), lambda bi, hi, i: (bi, i, hi))
    kvspec = pl.BlockSpec((1, s, ATT_V_DIM), lambda bi, hi, i: (bi, 0, hi))
    small = [lq1, lk1, lq2, lk2, sw]
    stat = pltpu.VMEM((t, LANES), F32)
    acc = pltpu.VMEM((t, 2 * ATT_V_DIM), F32)
    return pl.pallas_call(
        functools.partial(_attn_kernel, t=t, lam_init=lam_init),
        grid=(b, N_ATT_HEADS, s // t),
        in_specs=[qspec, kvspec, kvspec] + [_const_spec(a.shape) for a in small],
        out_specs=qspec, out_shape=jax.ShapeDtypeStruct((b, s, ATT_WIDTH), BF16),
        scratch_shapes=[pltpu.VMEM((s, 2 * ATT_V_DIM), BF16), stat, acc, stat, acc],
        compiler_params=_params(("arbitrary", "arbitrary", "arbitrary")),
        name="diff_attn")(q, k, v, *small)


def _expand_heads(v):
    r = v.shape[0]
    lane = lax.broadcasted_iota(jnp.int32, (r, LANES), 1)
    blocks = []
    for b in range(SSM_WIDTH // LANES):
        lo = jnp.broadcast_to(v[:, 2 * b:2 * b + 1], (r, LANES))
        hi = jnp.broadcast_to(v[:, 2 * b + 1:2 * b + 2], (r, LANES))
        blocks.append(jnp.where(lane < SSM_HEAD_DIM, lo, hi))
    return jnp.concatenate(blocks, axis=1)


def _ssd_kernel(xs_ref, b_ref, c_ref, dt_ref, z_ref, alog_ref, dexp_ref, nw_ref, y_ref, state,
                *, tc, tiles_per_seq):
    i = pl.program_id(0)
    t = SSM_CHUNK
    gw = SSM_WIDTH // SSM_GROUPS
    hpg = SSM_HEADS // SSM_GROUPS

    @pl.when(i % tiles_per_seq == 0)
    def _():
        state[...] = jnp.zeros(state.shape, F32)

    a_head = -jnp.exp(alog_ref[...])
    rr = lax.broadcasted_iota(jnp.int32, (t, t), 0)
    cc = lax.broadcasted_iota(jnp.int32, (t, t), 1)
    tril = rr >= cc
    tri = tril.astype(BF16)
    lo_half = cc < SSM_HEAD_DIM

    def chunk(ci, carry):
        r0 = pl.multiple_of(ci * t, t)
        rows = pl.ds(r0, t)
        xs = xs_ref[rows, :].astype(F32)
        dt = dt_ref[rows, :]
        a = dt * a_head
        p0, p1, p2 = _split3(a)
        acum = _dot(tri, p0) + _dot(tri, p1) + _dot(tri, p2)
        acum_t = acum.T
        alast = acum[t - 1:t, :]
        dtx = _expand_heads(dt)
        ea = _expand_heads(jnp.exp(acum))
        dec = _expand_heads(jnp.exp(alast - acum))
        elast = _expand_heads(jnp.exp(alast))
        x = xs * dtx
        xb = x.astype(BF16)
        xdec = (x * dec).astype(BF16)
        ys = []
        for g in range(SSM_GROUPS):
            gs = slice(g * gw, (g + 1) * gw)
            bg = b_ref[rows, g * SSM_STATE:(g + 1) * SSM_STATE]
            cg = c_ref[rows, g * SSM_STATE:(g + 1) * SSM_STATE]
            cb = _dot_nt(cg, bg)
            st = state[g]
            y_g = _dot(cg, st.astype(BF16)) * ea[:, gs]
            pairs = []
            for pr in range(hpg // 2):
                xp = xb[:, g * gw + pr * LANES:g * gw + (pr + 1) * LANES]
                res = []
                for hh in range(2):
                    hd = g * hpg + 2 * pr + hh
                    diff = acum[:, hd:hd + 1] - acum_t[hd:hd + 1, :]
                    lm = jnp.exp(jnp.where(tril, diff, -jnp.inf))
                    res.append(_dot((cb * lm).astype(BF16), xp))
                pairs.append(jnp.where(lo_half, res[0], res[1]))
            ys.append(y_g + jnp.concatenate(pairs, axis=1))
            state[g] = st * elast[:, gs] + _dot_tn(bg, xdec[:, gs])
        y = jnp.concatenate(ys, axis=1) + xs * dexp_ref[...]
        y = y * z_ref[rows, :].astype(F32)
        outs = []
        for g in range(SSM_GROUPS):
            gs = slice(g * gw, (g + 1) * gw)
            outs.append(_rms(y[:, gs], nw_ref[:, gs]))
        y_ref[rows, :] = jnp.concatenate(outs, axis=1).astype(BF16)
        return carry

    lax.fori_loop(0, tc // t, chunk, 0)


def _ssd(xbc, dt, zs, alog, dexp, nw, *, seq):
    m = xbc.shape[0]
    tc = SSD_TILE
    bc_w = SSM_GROUPS * SSM_STATE
    ins = [xbc, xbc, xbc, dt, zs, alog, dexp, nw]
    in_specs = [pl.BlockSpec((tc, SSM_WIDTH), lambda i: (i, 0)),
                pl.BlockSpec((tc, bc_w), lambda i: (i, SSM_WIDTH // bc_w)),
                pl.BlockSpec((tc, bc_w), lambda i: (i, SSM_WIDTH // bc_w + 1)),
                pl.BlockSpec((tc, LANES), lambda i: (i, 0)),
                pl.BlockSpec((tc, SSM_WIDTH), lambda i: (i, 0))] + [
                    _const_spec(a.shape) for a in ins[5:]]
    return pl.pallas_call(
        functools.partial(_ssd_kernel, tc=tc, tiles_per_seq=seq // tc),
        grid=(m // tc,), in_specs=in_specs,
        out_specs=pl.BlockSpec((tc, SSM_WIDTH), lambda i: (i, 0)),
        out_shape=jax.ShapeDtypeStruct((m, SSM_WIDTH), BF16),
        scratch_shapes=[pltpu.VMEM((SSM_GROUPS, SSM_STATE, SSM_WIDTH // SSM_GROUPS), F32)],
        compiler_params=_params(("arbitrary",)), name="ssd")(*ins)


def _ffn_body(x, nw_ref, upg_ref, upv_ref, cwg_ref, cbg_ref, cwv_ref, cbv_ref, down_ref,
              o_ref, gbuf, vbuf, abuf, act, *, tm, tiles_per_seq):
    i = pl.program_id(0)
    h = _rms(x, nw_ref[...]).astype(BF16)
    nchunk = FFN_DIM // LANES
    per_dot = MXU_DIM // LANES

    @pl.when(i % tiles_per_seq == 0)
    def _():
        gbuf[:, 0:SUBLANES, :] = jnp.zeros((nchunk, SUBLANES, LANES), F32)
        vbuf[:, 0:SUBLANES, :] = jnp.zeros((nchunk, SUBLANES, LANES), F32)

    for d in range(FFN_DIM // MXU_DIM):
        sl = slice(d * MXU_DIM, (d + 1) * MXU_DIM)
        rg = _dot(h, upg_ref[:, sl])
        rv = _dot(h, upv_ref[:, sl])
        for u in range(per_dot):
            c = d * per_dot + u
            gbuf[c, SUBLANES:SUBLANES + tm, :] = rg[:, _lane_chunk(u)]
            vbuf[c, SUBLANES:SUBLANES + tm, :] = rv[:, _lane_chunk(u)]
        for u in range(per_dot):
            c = d * per_dot + u
            for row in _strided_rows(tm):
                g = _conv_rows(gbuf, c, cwg_ref, cbg_ref, row, SUBLANES, FFN_CONV)
                v = _conv_rows(vbuf, c, cwv_ref, cbv_ref, row, SUBLANES, FFN_CONV)
                abuf[c, pl.ds(row, SUBLANES, stride=CONV_STRIDE), :] = _silu(g) * v
            act[:, _lane_chunk(c)] = abuf[c].astype(BF16)
            gbuf[c, 0:SUBLANES, :] = gbuf[c, tm:tm + SUBLANES, :]
            vbuf[c, 0:SUBLANES, :] = vbuf[c, tm:tm + SUBLANES, :]

    o_ref[...] = x + _dot(act[...], down_ref[...])


def _ffn_kernel(x_ref, *rest, **kw):
    _ffn_body(x_ref[...], *rest, **kw)


def _mix_ffn_kernel(x_ref, att_ref, y_ref, wa_ref, wy_ref, *rest, **kw):
    x = x_ref[...] + _dot(att_ref[...], wa_ref[...]) + _dot(y_ref[...], wy_ref[...])
    _ffn_body(x, *rest, **kw)


def _ffn(x2d, nw, upg, upv, cwg, cbg, cwv, cbv, down, *, seq, mix=None):
    m = x2d.shape[0]
    tm = FFN_ROW_TILE
    nchunk = FFN_DIM // LANES
    row = lambda c: pl.BlockSpec((tm, c), lambda i: (i, 0))
    consts = [nw, upg, upv, cwg, cbg, cwv, cbv, down]
    if mix is None:
        body, ins, specs = _ffn_kernel, [x2d], [row(D_MODEL)]
    else:
        att, y, wa, wy = mix
        body, ins = _mix_ffn_kernel, [x2d, att, y, wa, wy]
        specs = [row(D_MODEL), row(ATT_WIDTH), row(SSM_WIDTH),
                 _const_spec(wa.shape), _const_spec(wy.shape)]
    return pl.pallas_call(
        functools.partial(body, tm=tm, tiles_per_seq=seq // tm),
        grid=(m // tm,), in_specs=specs + [_const_spec(a.shape) for a in consts],
        out_specs=row(D_MODEL), out_shape=jax.ShapeDtypeStruct((m, D_MODEL), F32),
        scratch_shapes=[pltpu.VMEM((nchunk, tm + SUBLANES, LANES), F32),
                        pltpu.VMEM((nchunk, tm + SUBLANES, LANES), F32),
                        pltpu.VMEM((nchunk, tm, LANES), F32),
                        pltpu.VMEM((tm, FFN_DIM), BF16)],
        compiler_params=_params(("arbitrary",)), name="conv_ffn")(*ins, *consts)


def _conformer_kernel(x_ref, nw_ref, w1a_ref, w1b_ref, b1a_ref, b1b_ref, dww_ref, dwb_ref,
                      lnw_ref, lnb_ref, w2_ref, b2_ref, o_ref, cbuf, ybuf, *, tm, tiles_per_seq):
    i = pl.program_id(0)
    x = x_ref[...]
    h = _rms(x, nw_ref[...]).astype(BF16)
    nchunk = CONF_WIDTH // LANES

    @pl.when(i % tiles_per_seq == 0)
    def _():
        cbuf[:, 0:CONF_HEAD, :] = jnp.zeros((nchunk, CONF_HEAD, LANES), F32)

    ua = _dot(h, w1a_ref[...]) + b1a_ref[...]
    ub = _dot(h, w1b_ref[...]) + b1b_ref[...]
    glu = ua * _sigmoid(ub)
    for c in range(nchunk):
        cbuf[c, CONF_HEAD:CONF_HEAD + tm, :] = glu[:, _lane_chunk(c)]
    for c in range(nchunk):
        for row in _strided_rows(tm):
            ybuf[c, pl.ds(row, SUBLANES, stride=CONV_STRIDE), :] = _conv_rows(
                cbuf, c, dww_ref, dwb_ref, row, CONF_HEAD, CONF_K)
        cbuf[c, 0:CONF_HEAD, :] = cbuf[c, tm:tm + CONF_HEAD, :]

    u = jnp.concatenate([ybuf[c] for c in range(nchunk)], axis=1)
    mu = jnp.mean(u, axis=-1, keepdims=True)
    d = u - mu
    var = jnp.mean(d * d, axis=-1, keepdims=True)
    y = _silu(d * lax.rsqrt(var + LN_EPS) * lnw_ref[...] + lnb_ref[...]).astype(BF16)
    o_ref[...] = x + _dot(y, w2_ref[...]) + b2_ref[...]


def _conformer(x2d, nw, w1a, w1b, b1a, b1b, dww, dwb, lnw, lnb, w2, b2, *, seq):
    m = x2d.shape[0]
    tm = FFN_ROW_TILE
    nchunk = CONF_WIDTH // LANES
    row = pl.BlockSpec((tm, D_MODEL), lambda i: (i, 0))
    ins = [x2d, nw, w1a, w1b, b1a, b1b, dww, dwb, lnw, lnb, w2, b2]
    return pl.pallas_call(
        functools.partial(_conformer_kernel, tm=tm, tiles_per_seq=seq // tm),
        grid=(m // tm,), in_specs=[row] + [_const_spec(a.shape) for a in ins[1:]],
        out_specs=row, out_shape=jax.ShapeDtypeStruct((m, D_MODEL), F32),
        scratch_shapes=[pltpu.VMEM((nchunk, tm + CONF_HEAD, LANES), F32),
                        pltpu.VMEM((nchunk, tm, LANES), F32)],
        compiler_params=_params(("arbitrary",)), name="conformer")(*ins)


def _row(v):
    return v.reshape(1, -1).astype(F32)


def _pad_lanes(v, width=LANES):
    return jnp.pad(v, ((0, 0), (0, width - v.shape[1])))


def _mixer_branches(x2d, seq, lam_init, mix_norm_w, w_in, q_norm_w, k_norm_w, lq1, lk1, lq2, lk2,
                    attn_subln_w, conv_w, conv_b, dt_bias, a_log, d_skip, ssm_norm_w):
    m = x2d.shape[0]
    bsz = m // seq
    o = 0
    cols = []
    for width in (ATT_WIDTH, ATT_WIDTH, ATT_WIDTH, SSM_WIDTH, XBC_WIDTH, SSM_HEADS):
        cols.append(w_in[:, o:o + width].astype(BF16))
        o += width
    wq, wk, wv, wz, wx, wdt = cols
    wdt = _pad_lanes(wdt)
    reps = ATT_WIDTH // ATT_HEAD_DIM
    grp_id = jnp.arange(ATT_WIDTH) // ATT_HEAD_DIM
    grp = (grp_id[:, None] == grp_id[None, :]).astype(BF16)
    q, k, v, zs, xbc, dt = _inproj(
        x2d, _row(mix_norm_w), wq, wk, wv, wz, wx, wdt,
        _row(jnp.tile(q_norm_w, reps)), _row(jnp.tile(k_norm_w, reps)), grp,
        conv_w.astype(F32), _row(conv_b), _pad_lanes(_row(dt_bias)), seq=seq)
    shp = (bsz, seq, ATT_WIDTH)
    att = _attention(q.reshape(shp), k.reshape(shp), v.reshape(shp),
                     _row(lq1), _row(lk1), _row(lq2), _row(lk2), _row(attn_subln_w),
                     lam_init=lam_init).reshape(m, ATT_WIDTH)
    y = _ssd(xbc, dt, zs, _pad_lanes(_row(a_log)), _row(jnp.repeat(d_skip, SSM_HEAD_DIM)),
             _row(ssm_norm_w), seq=seq)
    return att, y


def _ffn_layer(x2d, seq, norm_w, up_w, conv_w, conv_b, down_w, mix=None):
    up = up_w.astype(BF16)
    cw = conv_w.astype(F32)
    cb = _row(conv_b)
    return _ffn(x2d, _row(norm_w), up[:, :FFN_DIM], up[:, FFN_DIM:],
                cw[:, :FFN_DIM], cb[:, :FFN_DIM], cw[:, FFN_DIM:], cb[:, FFN_DIM:],
                down_w.astype(BF16), seq=seq, mix=mix)


def _conformer_layer(x2d, seq, norm_w, pw1_w, pw1_b, dw_w, dw_b, ln_w, ln_b, pw2_w, pw2_b):
    w1 = pw1_w.astype(BF16)
    b1 = _row(pw1_b)
    return _conformer(x2d, _row(norm_w), w1[:, :CONF_WIDTH], w1[:, CONF_WIDTH:],
                      b1[:, :CONF_WIDTH], b1[:, CONF_WIDTH:], dw_w.astype(F32), _row(dw_b),
                      _row(ln_w), _row(ln_b), pw2_w.astype(BF16), _row(pw2_b), seq=seq)


def kernel(x, mix_norm_w, w_in, q_norm_w, k_norm_w, lambda_q1, lambda_k1, lambda_q2, lambda_k2,
           attn_subln_w, ssm_conv_w, ssm_conv_b, ssm_dt_bias, ssm_A_log, ssm_D, ssm_norm_w, w_out,
           conf_norm_w, conf_pw1_w, conf_pw1_b, conf_dw_w, conf_dw_b, conf_ln_w, conf_ln_b,
           conf_pw2_w, conf_pw2_b, ffn_norm_w, ffn_up_w, ffn_conv_w, ffn_conv_b, ffn_down_w):
    bsz, seq, d = x.shape
    depth = ffn_norm_w.shape[0]
    h = x.reshape(bsz * seq, d)
    for i in range(depth):
        mix = None
        if i % 2 == 0:
            e = i // 2
            lam_init = 0.8 - 0.6 * math.exp(-0.3 * i)
            att, y = _mixer_branches(
                h, seq, lam_init, mix_norm_w[e], w_in[e], q_norm_w[e], k_norm_w[e],
                lambda_q1[e], lambda_k1[e], lambda_q2[e], lambda_k2[e], attn_subln_w[e],
                ssm_conv_w[e], ssm_conv_b[e], ssm_dt_bias[e], ssm_A_log[e], ssm_D[e],
                ssm_norm_w[e])
            wo = w_out[e].astype(BF16)
            mix = (att, y, wo[:ATT_WIDTH], wo[ATT_WIDTH:])
        else:
            o = i // 2
            h = _conformer_layer(h, seq, conf_norm_w[o], conf_pw1_w[o], conf_pw1_b[o],
                                 conf_dw_w[o], conf_dw_b[o], conf_ln_w[o], conf_ln_b[o],
                                 conf_pw2_w[o], conf_pw2_b[o])
        h = _ffn_layer(h, seq, ffn_norm_w[i], ffn_up_w[i], ffn_conv_w[i], ffn_conv_b[i],
                       ffn_down_w[i], mix=mix)
    return h.reshape(bsz, seq, d)
```

```python
import functools
import math

import jax
import jax.numpy as jnp
from jax import lax
from jax.experimental import pallas as pl
from jax.experimental.pallas import tpu as pltpu

F32 = jnp.float32
BF16 = jnp.bfloat16

D_MODEL = 1024
N_ATT_HEADS = 4
ATT_HEAD_DIM = 64
ATT_V_DIM = 2 * ATT_HEAD_DIM
ATT_WIDTH = N_ATT_HEADS * ATT_V_DIM
ATT_SCALE = ATT_HEAD_DIM ** -0.5
SSM_WIDTH = D_MODEL - ATT_WIDTH
SSM_HEAD_DIM = 64
SSM_HEADS = SSM_WIDTH // SSM_HEAD_DIM
SSM_GROUPS = 2
SSM_STATE = 128
SSM_CONV = 4
SSM_CHUNK = 128
XBC_WIDTH = SSM_WIDTH + 2 * SSM_GROUPS * SSM_STATE
CONF_WIDTH = D_MODEL
CONF_K = 31
FFN_DIM = 2816
FFN_CONV = 3
RMS_EPS = 1e-6
LN_EPS = 1e-5

LANES = 128
SUBLANES = 8
MXU_DIM = 256
VMEM_LIMIT = 56 * 1024 * 1024

ROW_TILE = 512
FFN_ROW_TILE = 512
CONF_ROW_TILE = 256
ATT_TILE = 512
SSD_TILE = 512
CONV_STRIDE = 4
CONV_BLOCK = SUBLANES * CONV_STRIDE
CONF_HEAD = 32


def _dot(a, b):
    return jnp.dot(a, b, preferred_element_type=F32)


def _dot_nt(a, b):
    return lax.dot_general(a, b, (((1,), (1,)), ((), ())), preferred_element_type=F32)


def _dot_tn(a, b):
    return lax.dot_general(a, b, (((0,), (0,)), ((), ())), preferred_element_type=F32)


def _rms(x, w):
    ms = jnp.mean(x * x, axis=-1, keepdims=True)
    return x * lax.rsqrt(ms + RMS_EPS) * w


def _sigmoid(x):
    return 1.0 / (1.0 + jnp.exp(-x))


def _silu(x):
    return x * _sigmoid(x)


def _softplus(x):
    return jnp.maximum(x, 0.0) + jnp.log(1.0 + jnp.exp(-jnp.abs(x)))


def _split3(x):
    a = x.astype(BF16)
    r = x - a.astype(F32)
    b = r.astype(BF16)
    c = (r - b.astype(F32)).astype(BF16)
    return a, b, c


def _const_spec(shape):
    nd = len(shape)
    return pl.BlockSpec(shape, lambda *_: (0,) * nd, pipeline_mode=pl.Buffered(1))


def _params(sem):
    return pltpu.CompilerParams(dimension_semantics=sem, vmem_limit_bytes=VMEM_LIMIT)


def _lane_chunk(c):
    return slice(c * LANES, (c + 1) * LANES)


def _conv_rows(buf, c, w_ref, b_ref, wc, row, head, ntaps):
    first = head - (ntaps - 1) + row
    acc = b_ref[wc] + w_ref[wc, 0:1, :] * buf[c, pl.ds(first, SUBLANES, stride=CONV_STRIDE), :]
    for k in range(1, ntaps):
        acc = acc + w_ref[wc, k:k + 1, :] * buf[c, pl.ds(first + k, SUBLANES, stride=CONV_STRIDE), :]
    return acc


def _strided_rows(tm):
    return [blk * CONV_BLOCK + b for blk in range(tm // CONV_BLOCK) for b in range(CONV_STRIDE)]


def _inproj_kernel(x_ref, nw_ref, w_ref, wdt_ref, qnw_ref, knw_ref, grp_ref, cw_ref, cb_ref,
                   dtb_ref, q_ref, k_ref, v_ref, z_ref, xbc_ref, dt_ref, cbuf, ybuf,
                   *, tm, tiles_per_seq):
    i = pl.program_id(0)
    h = _rms(x_ref[...], nw_ref[...]).astype(BF16)
    nchunk = XBC_WIDTH // LANES
    col_q, col_k, col_v, col_z, col_x = (
        0, ATT_WIDTH, 2 * ATT_WIDTH, 3 * ATT_WIDTH, 3 * ATT_WIDTH + SSM_WIDTH)

    @pl.when(i % tiles_per_seq == 0)
    def _():
        cbuf[:, 0:SUBLANES, :] = jnp.zeros((nchunk, SUBLANES, LANES), F32)

    raw = _dot(h, w_ref[:, col_x:col_x + XBC_WIDTH])
    for c in range(nchunk):
        cbuf[c, SUBLANES:SUBLANES + tm, :] = raw[:, _lane_chunk(c)]
    for c in range(nchunk):
        for row in _strided_rows(tm):
            acc = _conv_rows(cbuf, c, cw_ref, cb_ref, c, row, SUBLANES, SSM_CONV)
            ybuf[c, pl.ds(row, SUBLANES, stride=CONV_STRIDE), :] = _silu(acc)
        xbc_ref[:, _lane_chunk(c)] = ybuf[c].astype(BF16)
        cbuf[c, 0:SUBLANES, :] = cbuf[c, tm:tm + SUBLANES, :]

    def qk_norm(col, nw):
        p = _dot(h, w_ref[:, col:col + ATT_WIDTH])
        s_hi, s_lo, _ = _split3(p * p)
        ms = (_dot(s_hi, grp_ref[...]) + _dot(s_lo, grp_ref[...])) * (1.0 / ATT_HEAD_DIM)
        return p * lax.rsqrt(ms + RMS_EPS) * nw

    q_ref[...] = (qk_norm(col_q, qnw_ref[...]) * ATT_SCALE).astype(BF16)
    k_ref[...] = qk_norm(col_k, knw_ref[...]).astype(BF16)
    v_ref[...] = _dot(h, w_ref[:, col_v:col_v + ATT_WIDTH]).astype(BF16)
    z_ref[...] = _silu(_dot(h, w_ref[:, col_z:col_z + SSM_WIDTH])).astype(BF16)
    dt_ref[...] = _softplus(_dot(h, wdt_ref[...]) + dtb_ref[...])


def _inproj(x2d, nw, w, wdt, qnw, knw, grp, cw, cb, dtb, *, seq):
    m = x2d.shape[0]
    tm = ROW_TILE
    row = lambda c: pl.BlockSpec((tm, c), lambda i: (i, 0))
    ins = [x2d, nw, w, wdt, qnw, knw, grp, cw, cb, dtb]
    in_specs = [row(D_MODEL)] + [_const_spec(a.shape) for a in ins[1:]]
    out_shape = [jax.ShapeDtypeStruct((m, ATT_WIDTH), BF16)] * 3 + [
        jax.ShapeDtypeStruct((m, SSM_WIDTH), BF16),
        jax.ShapeDtypeStruct((m, XBC_WIDTH), BF16),
        jax.ShapeDtypeStruct((m, LANES), F32)]
    out_specs = [row(ATT_WIDTH)] * 3 + [row(SSM_WIDTH), row(XBC_WIDTH), row(LANES)]
    nchunk = XBC_WIDTH // LANES
    return pl.pallas_call(
        functools.partial(_inproj_kernel, tm=tm, tiles_per_seq=seq // tm),
        grid=(m // tm,), in_specs=in_specs, out_specs=out_specs, out_shape=out_shape,
        scratch_shapes=[pltpu.VMEM((nchunk, tm + SUBLANES, LANES), F32),
                        pltpu.VMEM((nchunk, tm, LANES), F32)],
        compiler_params=_params(("arbitrary",)), name="inproj")(*ins)


def _attn_kernel(q_ref, k_ref, v_ref, lq1_ref, lk1_ref, lq2_ref, lk2_ref, sw_ref, o_ref,
                 vext, mst, acc, *, t, lam_init):
    i = pl.program_id(1)
    nh = N_ATT_HEADS
    head = lambda hh: slice(hh * ATT_V_DIM, (hh + 1) * ATT_V_DIM)

    @pl.when(i == 0)
    def _():
        for hh in range(nh):
            vext[hh, :, 0:ATT_V_DIM] = v_ref[0, :, head(hh)]
            vext[hh, :, ATT_V_DIM:] = jnp.ones((vext.shape[1], ATT_V_DIM), BF16)

    lane = lax.broadcasted_iota(jnp.int32, (t, ATT_V_DIM), 1)
    qs = []
    for hh in range(nh):
        q = q_ref[0, :, head(hh)]
        zero = jnp.zeros_like(q)
        qs.append(jnp.where(lane < ATT_HEAD_DIM, q, zero))
        qs.append(jnp.where(lane >= ATT_HEAD_DIM, q, zero))

    mst[...] = jnp.full(mst.shape, -1e30, F32)
    acc[...] = jnp.zeros(acc.shape, F32)

    def update(s, vb, idx):
        m_old = mst[idx]
        m_new = jnp.maximum(m_old, jnp.max(s, axis=-1, keepdims=True))
        alpha = jnp.exp(m_old - m_new)
        p = jnp.exp(s - jnp.tile(m_new, (1, t // LANES)))
        acc[idx] = jnp.tile(alpha, (1, 2)) * acc[idx] + _dot(p.astype(BF16), vb)
        mst[idx] = m_new

    def block(j, mask):
        r0 = pl.multiple_of(j * t, t)
        for hh in range(nh):
            kb = k_ref[0, pl.ds(r0, t), head(hh)]
            vb = vext[hh, pl.ds(r0, t), :]
            for c in range(2):
                s = _dot_nt(qs[2 * hh + c], kb)
                if mask is not None:
                    s = jnp.where(mask, s, -jnp.inf)
                update(s, vb, 2 * hh + c)

    def body(j, carry):
        block(j, None)
        return carry

    lax.fori_loop(0, i, body, 0)
    rr = lax.broadcasted_iota(jnp.int32, (t, t), 0)
    cc = lax.broadcasted_iota(jnp.int32, (t, t), 1)
    block(i, rr >= cc)

    lam = (jnp.exp(jnp.sum(lq1_ref[...] * lk1_ref[...], axis=-1, keepdims=True))
           - jnp.exp(jnp.sum(lq2_ref[...] * lk2_ref[...], axis=-1, keepdims=True)) + lam_init)
    for hh in range(nh):
        a1 = acc[2 * hh]
        a2 = acc[2 * hh + 1]
        o = (a1[:, 0:ATT_V_DIM] / a1[:, ATT_V_DIM:]
             - lam * (a2[:, 0:ATT_V_DIM] / a2[:, ATT_V_DIM:]))
        o_ref[0, :, head(hh)] = (_rms(o, sw_ref[...]) * (1.0 - lam_init)).astype(BF16)


def _attention(q, k, v, lq1, lk1, lq2, lk2, sw, *, lam_init):
    b, s, _ = q.shape
    t = ATT_TILE
    qspec = pl.BlockSpec((1, t, ATT_WIDTH), lambda bi, i: (bi, i, 0))
    kvspec = pl.BlockSpec((1, s, ATT_WIDTH), lambda bi, i: (bi, 0, 0))
    small = [lq1, lk1, lq2, lk2, sw]
    return pl.pallas_call(
        functools.partial(_attn_kernel, t=t, lam_init=lam_init),
        grid=(b, s // t),
        in_specs=[qspec, kvspec, kvspec] + [_const_spec(a.shape) for a in small],
        out_specs=qspec, out_shape=jax.ShapeDtypeStruct((b, s, ATT_WIDTH), BF16),
        scratch_shapes=[pltpu.VMEM((N_ATT_HEADS, s, 2 * ATT_V_DIM), BF16),
                        pltpu.VMEM((2 * N_ATT_HEADS, t, LANES), F32),
                        pltpu.VMEM((2 * N_ATT_HEADS, t, 2 * ATT_V_DIM), F32)],
        compiler_params=_params(("arbitrary", "arbitrary")),
        name="diff_attn")(q, k, v, *small)


def _expand_heads(v):
    r = v.shape[0]
    lane = lax.broadcasted_iota(jnp.int32, (r, LANES), 1)
    blocks = []
    for b in range(SSM_WIDTH // LANES):
        lo = jnp.broadcast_to(v[:, 2 * b:2 * b + 1], (r, LANES))
        hi = jnp.broadcast_to(v[:, 2 * b + 1:2 * b + 2], (r, LANES))
        blocks.append(jnp.where(lane < SSM_HEAD_DIM, lo, hi))
    return jnp.concatenate(blocks, axis=1)


def _ssd_kernel(xs_ref, b_ref, c_ref, dt_ref, z_ref, alog_ref, dexp_ref, nw_ref, y_ref, state,
                *, tc, tiles_per_seq):
    i = pl.program_id(0)
    t = SSM_CHUNK
    gw = SSM_WIDTH // SSM_GROUPS
    hpg = SSM_HEADS // SSM_GROUPS

    @pl.when(i % tiles_per_seq == 0)
    def _():
        state[...] = jnp.zeros(state.shape, F32)

    a_head = -jnp.exp(alog_ref[...])
    rr = lax.broadcasted_iota(jnp.int32, (t, t), 0)
    cc = lax.broadcasted_iota(jnp.int32, (t, t), 1)
    tril = rr >= cc
    tri = tril.astype(BF16)
    lo_half = cc < SSM_HEAD_DIM

    def chunk(ci, carry):
        r0 = pl.multiple_of(ci * t, t)
        rows = pl.ds(r0, t)
        xs = xs_ref[rows, :].astype(F32)
        dt = dt_ref[rows, :]
        a = dt * a_head
        p0, p1, p2 = _split3(a)
        acum = _dot(tri, p0) + _dot(tri, p1) + _dot(tri, p2)
        acum_t = acum.T
        alast = acum[t - 1:t, :]
        dtx = _expand_heads(dt)
        ea = _expand_heads(jnp.exp(acum))
        dec = _expand_heads(jnp.exp(alast - acum))
        elast = _expand_heads(jnp.exp(alast))
        x = xs * dtx
        xb = x.astype(BF16)
        xdec = (x * dec).astype(BF16)
        ys = []
        for g in range(SSM_GROUPS):
            gs = slice(g * gw, (g + 1) * gw)
            bg = b_ref[rows, g * SSM_STATE:(g + 1) * SSM_STATE]
            cg = c_ref[rows, g * SSM_STATE:(g + 1) * SSM_STATE]
            cb = _dot_nt(cg, bg)
            st = state[g]
            y_g = _dot(cg, st.astype(BF16)) * ea[:, gs]
            pairs = []
            for pr in range(hpg // 2):
                xp = xb[:, g * gw + pr * LANES:g * gw + (pr + 1) * LANES]
                res = []
                for hh in range(2):
                    hd = g * hpg + 2 * pr + hh
                    diff = acum[:, hd:hd + 1] - acum_t[hd:hd + 1, :]
                    lm = jnp.exp(jnp.where(tril, diff, -jnp.inf))
                    res.append(_dot((cb * lm).astype(BF16), xp))
                pairs.append(jnp.where(lo_half, res[0], res[1]))
            ys.append(y_g + jnp.concatenate(pairs, axis=1))
            state[g] = st * elast[:, gs] + _dot_tn(bg, xdec[:, gs])
        y = jnp.concatenate(ys, axis=1) + xs * dexp_ref[...]
        y = y * z_ref[rows, :].astype(F32)
        outs = []
        for g in range(SSM_GROUPS):
            gs = slice(g * gw, (g + 1) * gw)
            outs.append(_rms(y[:, gs], nw_ref[:, gs]))
        y_ref[rows, :] = jnp.concatenate(outs, axis=1).astype(BF16)
        return carry

    lax.fori_loop(0, tc // t, chunk, 0)


def _ssd(xbc, dt, zs, alog, dexp, nw, *, seq):
    m = xbc.shape[0]
    tc = SSD_TILE
    bc_w = SSM_GROUPS * SSM_STATE
    ins = [xbc, xbc, xbc, dt, zs, alog, dexp, nw]
    in_specs = [pl.BlockSpec((tc, SSM_WIDTH), lambda i: (i, 0)),
                pl.BlockSpec((tc, bc_w), lambda i: (i, SSM_WIDTH // bc_w)),
                pl.BlockSpec((tc, bc_w), lambda i: (i, SSM_WIDTH // bc_w + 1)),
                pl.BlockSpec((tc, LANES), lambda i: (i, 0)),
                pl.BlockSpec((tc, SSM_WIDTH), lambda i: (i, 0))] + [
                    _const_spec(a.shape) for a in ins[5:]]
    return pl.pallas_call(
        functools.partial(_ssd_kernel, tc=tc, tiles_per_seq=seq // tc),
        grid=(m // tc,), in_specs=in_specs,
        out_specs=pl.BlockSpec((tc, SSM_WIDTH), lambda i: (i, 0)),
        out_shape=jax.ShapeDtypeStruct((m, SSM_WIDTH), BF16),
        scratch_shapes=[pltpu.VMEM((SSM_GROUPS, SSM_STATE, SSM_WIDTH // SSM_GROUPS), F32)],
        compiler_params=_params(("arbitrary",)), name="ssd")(*ins)


def _ffn_body(x, nw_ref, up_ref, cw_ref, cb_ref, down_ref,
              o_ref, gbuf, vbuf, abuf, act, *, tm, tiles_per_seq):
    i = pl.program_id(0)
    h = _rms(x, nw_ref[...]).astype(BF16)
    nchunk = FFN_DIM // LANES
    per_dot = MXU_DIM // LANES

    @pl.when(i % tiles_per_seq == 0)
    def _():
        gbuf[:, 0:SUBLANES, :] = jnp.zeros((nchunk, SUBLANES, LANES), F32)
        vbuf[:, 0:SUBLANES, :] = jnp.zeros((nchunk, SUBLANES, LANES), F32)

    for d in range(FFN_DIM // MXU_DIM):
        c0 = d * MXU_DIM
        rg = _dot(h, up_ref[:, c0:c0 + MXU_DIM])
        rv = _dot(h, up_ref[:, FFN_DIM + c0:FFN_DIM + c0 + MXU_DIM])
        for u in range(per_dot):
            c = d * per_dot + u
            gbuf[c, SUBLANES:SUBLANES + tm, :] = rg[:, _lane_chunk(u)]
            vbuf[c, SUBLANES:SUBLANES + tm, :] = rv[:, _lane_chunk(u)]
        for u in range(per_dot):
            c = d * per_dot + u
            for row in _strided_rows(tm):
                g = _conv_rows(gbuf, c, cw_ref, cb_ref, c, row, SUBLANES, FFN_CONV)
                v = _conv_rows(vbuf, c, cw_ref, cb_ref, nchunk + c, row, SUBLANES, FFN_CONV)
                abuf[c, pl.ds(row, SUBLANES, stride=CONV_STRIDE), :] = _silu(g) * v
            act[:, _lane_chunk(c)] = abuf[c].astype(BF16)
            gbuf[c, 0:SUBLANES, :] = gbuf[c, tm:tm + SUBLANES, :]
            vbuf[c, 0:SUBLANES, :] = vbuf[c, tm:tm + SUBLANES, :]

    o_ref[...] = x + _dot(act[...], down_ref[...])


def _ffn_kernel(x_ref, *rest, **kw):
    _ffn_body(x_ref[...], *rest, **kw)


def _mix_ffn_kernel(x_ref, att_ref, y_ref, wo_ref, *rest, **kw):
    x = (x_ref[...] + _dot(att_ref[...], wo_ref[0:ATT_WIDTH, :])
         + _dot(y_ref[...], wo_ref[ATT_WIDTH:, :]))
    _ffn_body(x, *rest, **kw)


def _ffn(x2d, nw, up, cw, cb, down, *, seq, mix=None):
    m = x2d.shape[0]
    tm = FFN_ROW_TILE
    nchunk = FFN_DIM // LANES
    row = lambda c: pl.BlockSpec((tm, c), lambda i: (i, 0))
    consts = [nw, up, cw, cb, down]
    if mix is None:
        body, ins, specs = _ffn_kernel, [x2d], [row(D_MODEL)]
    else:
        att, y, wo = mix
        body, ins = _mix_ffn_kernel, [x2d, att, y, wo]
        specs = [row(D_MODEL), row(ATT_WIDTH), row(SSM_WIDTH), _const_spec(wo.shape)]
    return pl.pallas_call(
        functools.partial(body, tm=tm, tiles_per_seq=seq // tm),
        grid=(m // tm,), in_specs=specs + [_const_spec(a.shape) for a in consts],
        out_specs=row(D_MODEL), out_shape=jax.ShapeDtypeStruct((m, D_MODEL), F32),
        scratch_shapes=[pltpu.VMEM((nchunk, tm + SUBLANES, LANES), F32),
                        pltpu.VMEM((nchunk, tm + SUBLANES, LANES), F32),
                        pltpu.VMEM((nchunk, tm, LANES), F32),
                        pltpu.VMEM((tm, FFN_DIM), BF16)],
        compiler_params=_params(("arbitrary",)), name="conv_ffn")(*ins, *consts)


def _conformer_kernel(x_ref, nw_ref, w1_ref, b1_ref, dww_ref, dwb_ref,
                      lnw_ref, lnb_ref, w2_ref, b2_ref, o_ref, cbuf, ybuf, *, tm, tiles_per_seq):
    i = pl.program_id(0)
    x = x_ref[...]
    h = _rms(x, nw_ref[...]).astype(BF16)
    nchunk = CONF_WIDTH // LANES

    @pl.when(i % tiles_per_seq == 0)
    def _():
        cbuf[:, 0:CONF_HEAD, :] = jnp.zeros((nchunk, CONF_HEAD, LANES), F32)

    ua = _dot(h, w1_ref[:, 0:CONF_WIDTH]) + b1_ref[:, 0:CONF_WIDTH]
    ub = _dot(h, w1_ref[:, CONF_WIDTH:]) + b1_ref[:, CONF_WIDTH:]
    glu = ua * _sigmoid(ub)
    for c in range(nchunk):
        cbuf[c, CONF_HEAD:CONF_HEAD + tm, :] = glu[:, _lane_chunk(c)]

    def conv_chunk(c, carry):
        for row in _strided_rows(tm):
            ybuf[c, pl.ds(row, SUBLANES, stride=CONV_STRIDE), :] = _conv_rows(
                cbuf, c, dww_ref, dwb_ref, c, row, CONF_HEAD, CONF_K)
        cbuf[c, 0:CONF_HEAD, :] = cbuf[c, tm:tm + CONF_HEAD, :]
        return carry

    lax.fori_loop(0, nchunk, conv_chunk, 0)

    u = jnp.concatenate([ybuf[c] for c in range(nchunk)], axis=1)
    mu = jnp.mean(u, axis=-1, keepdims=True)
    d = u - mu
    var = jnp.mean(d * d, axis=-1, keepdims=True)
    y = _silu(d * lax.rsqrt(var + LN_EPS) * lnw_ref[...] + lnb_ref[...]).astype(BF16)
    o_ref[...] = x + _dot(y, w2_ref[...]) + b2_ref[...]


def _conformer(x2d, nw, w1, b1, dww, dwb, lnw, lnb, w2, b2, *, seq):
    m = x2d.shape[0]
    tm = CONF_ROW_TILE
    nchunk = CONF_WIDTH // LANES
    row = pl.BlockSpec((tm, D_MODEL), lambda i: (i, 0))
    ins = [x2d, nw, w1, b1, dww, dwb, lnw, lnb, w2, b2]
    return pl.pallas_call(
        functools.partial(_conformer_kernel, tm=tm, tiles_per_seq=seq // tm),
        grid=(m // tm,), in_specs=[row] + [_const_spec(a.shape) for a in ins[1:]],
        out_specs=row, out_shape=jax.ShapeDtypeStruct((m, D_MODEL), F32),
        scratch_shapes=[pltpu.VMEM((nchunk, tm + CONF_HEAD, LANES), F32),
                        pltpu.VMEM((nchunk, tm, LANES), F32)],
        compiler_params=_params(("arbitrary",)), name="conformer")(*ins)


def _row(v):
    return v.reshape(1, -1).astype(F32)


def _chunk_taps(w):
    k, c = w.shape
    return w.astype(F32).reshape(k, c // LANES, LANES).transpose(1, 0, 2)


def _chunk_bias(b):
    return b.astype(F32).reshape(-1, 1, LANES)


def _pad_lanes(v, width=LANES):
    return jnp.pad(v, ((0, 0), (0, width - v.shape[1])))


def _mixer_branches(x2d, seq, lam_init, mix_norm_w, w_in, q_norm_w, k_norm_w, lq1, lk1, lq2, lk2,
                    attn_subln_w, conv_w, conv_b, dt_bias, a_log, d_skip, ssm_norm_w):
    m = x2d.shape[0]
    bsz = m // seq
    dt_col = 3 * ATT_WIDTH + SSM_WIDTH + XBC_WIDTH
    wdt = _pad_lanes(w_in[:, dt_col:].astype(BF16))
    reps = ATT_WIDTH // ATT_HEAD_DIM
    grp_id = jnp.arange(ATT_WIDTH) // ATT_HEAD_DIM
    grp = (grp_id[:, None] == grp_id[None, :]).astype(BF16)
    q, k, v, zs, xbc, dt = _inproj(
        x2d, _row(mix_norm_w), w_in.astype(BF16), wdt,
        _row(jnp.tile(q_norm_w, reps)), _row(jnp.tile(k_norm_w, reps)), grp,
        _chunk_taps(conv_w), _chunk_bias(conv_b), _pad_lanes(_row(dt_bias)), seq=seq)
    shp = (bsz, seq, ATT_WIDTH)
    att = _attention(q.reshape(shp), k.reshape(shp), v.reshape(shp),
                     _row(lq1), _row(lk1), _row(lq2), _row(lk2), _row(attn_subln_w),
                     lam_init=lam_init).reshape(m, ATT_WIDTH)
    y = _ssd(xbc, dt, zs, _pad_lanes(_row(a_log)), _row(jnp.repeat(d_skip, SSM_HEAD_DIM)),
             _row(ssm_norm_w), seq=seq)
    return att, y


def _ffn_layer(x2d, seq, norm_w, up_w, conv_w, conv_b, down_w, mix=None):
    return _ffn(x2d, _row(norm_w), up_w.astype(BF16), _chunk_taps(conv_w), _chunk_bias(conv_b),
                down_w.astype(BF16), seq=seq, mix=mix)


def _conformer_layer(x2d, seq, norm_w, pw1_w, pw1_b, dw_w, dw_b, ln_w, ln_b, pw2_w, pw2_b):
    return _conformer(x2d, _row(norm_w), pw1_w.astype(BF16), _row(pw1_b),
                      _chunk_taps(dw_w), _chunk_bias(dw_b),
                      _row(ln_w), _row(ln_b), pw2_w.astype(BF16), _row(pw2_b), seq=seq)


def kernel(x, mix_norm_w, w_in, q_norm_w, k_norm_w, lambda_q1, lambda_k1, lambda_q2, lambda_k2,
           attn_subln_w, ssm_conv_w, ssm_conv_b, ssm_dt_bias, ssm_A_log, ssm_D, ssm_norm_w, w_out,
           conf_norm_w, conf_pw1_w, conf_pw1_b, conf_dw_w, conf_dw_b, conf_ln_w, conf_ln_b,
           conf_pw2_w, conf_pw2_b, ffn_norm_w, ffn_up_w, ffn_conv_w, ffn_conv_b, ffn_down_w):
    bsz, seq, d = x.shape
    depth = ffn_norm_w.shape[0]
    h = x.reshape(bsz * seq, d)
    for i in range(depth):
        mix = None
        if i % 2 == 0:
            e = i // 2
            lam_init = 0.8 - 0.6 * math.exp(-0.3 * i)
            att, y = _mixer_branches(
                h, seq, lam_init, mix_norm_w[e], w_in[e], q_norm_w[e], k_norm_w[e],
                lambda_q1[e], lambda_k1[e], lambda_q2[e], lambda_k2[e], attn_subln_w[e],
                ssm_conv_w[e], ssm_conv_b[e], ssm_dt_bias[e], ssm_A_log[e], ssm_D[e],
                ssm_norm_w[e])
            mix = (att, y, w_out[e].astype(BF16))
        else:
            o = i // 2
            h = _conformer_layer(h, seq, conf_norm_w[o], conf_pw1_w[o], conf_pw1_b[o],
                                 conf_dw_w[o], conf_dw_b[o], conf_ln_w[o], conf_ln_b[o],
                                 conf_pw2_w[o], conf_pw2_b[o])
        h = _ffn_layer(h, seq, ffn_norm_w[i], ffn_up_w[i], ffn_conv_w[i], ffn_conv_b[i],
                       ffn_down_w[i], mix=mix)
    return h.reshape(bsz, seq, d)
```

```python
import functools
import math

import jax
import jax.numpy as jnp
from jax import lax
from jax.experimental import pallas as pl
from jax.experimental.pallas import tpu as pltpu

F32 = jnp.float32
BF16 = jnp.bfloat16

D_MODEL = 1024
N_ATT_HEADS = 4
ATT_HEAD_DIM = 64
ATT_V_DIM = 2 * ATT_HEAD_DIM
ATT_WIDTH = N_ATT_HEADS * ATT_V_DIM
ATT_SCALE = ATT_HEAD_DIM ** -0.5
LOG2_E = math.log2(math.e)
SSM_WIDTH = D_MODEL - ATT_WIDTH
SSM_HEAD_DIM = 64
SSM_HEADS = SSM_WIDTH // SSM_HEAD_DIM
SSM_GROUPS = 2
SSM_STATE = 128
SSM_CONV = 4
SSM_CHUNK = 128
XBC_WIDTH = SSM_WIDTH + 2 * SSM_GROUPS * SSM_STATE
CONF_WIDTH = D_MODEL
CONF_K = 31
FFN_DIM = 2816
FFN_CONV = 3
RMS_EPS = 1e-6
LN_EPS = 1e-5

LANES = 128
SUBLANES = 8
MXU_DIM = 256
VMEM_LIMIT = 56 * 1024 * 1024

ROW_TILE = 512
FFN_ROW_TILE = 512
CONF_ROW_TILE = 512
ATT_TILE = 512
SSD_TILE = 512
CONV_STRIDE = 4
CONV_BLOCK = SUBLANES * CONV_STRIDE
CONF_HEAD = 32


def _dot(a, b):
    return jnp.dot(a, b, preferred_element_type=F32)


def _dot_nt(a, b):
    return lax.dot_general(a, b, (((1,), (1,)), ((), ())), preferred_element_type=F32)


def _dot_tn(a, b):
    return lax.dot_general(a, b, (((0,), (0,)), ((), ())), preferred_element_type=F32)


def _rms(x, w):
    ms = jnp.mean(x * x, axis=-1, keepdims=True)
    return x * lax.rsqrt(ms + RMS_EPS) * w


def _sigmoid(x):
    return 1.0 / (1.0 + jnp.exp(-x))


def _silu(x):
    return x * _sigmoid(x)


def _softplus(x):
    return jnp.maximum(x, 0.0) + jnp.log(1.0 + jnp.exp(-jnp.abs(x)))


def _split3(x):
    a = x.astype(BF16)
    r = x - a.astype(F32)
    b = r.astype(BF16)
    c = (r - b.astype(F32)).astype(BF16)
    return a, b, c


def _const_spec(shape):
    nd = len(shape)
    return pl.BlockSpec(shape, lambda *_: (0,) * nd, pipeline_mode=pl.Buffered(1))


def _params(sem):
    return pltpu.CompilerParams(dimension_semantics=sem, vmem_limit_bytes=VMEM_LIMIT)


def _lane_chunk(c):
    return slice(c * LANES, (c + 1) * LANES)


def _conv_rows(buf, c, w_ref, b_ref, wc, row, head, ntaps):
    first = head - (ntaps - 1) + row
    acc = b_ref[wc] + w_ref[wc, 0:1, :] * buf[c, pl.ds(first, SUBLANES, stride=CONV_STRIDE), :]
    for k in range(1, ntaps):
        acc = acc + w_ref[wc, k:k + 1, :] * buf[c, pl.ds(first + k, SUBLANES, stride=CONV_STRIDE), :]
    return acc


def _strided_rows(tm):
    return [blk * CONV_BLOCK + b for blk in range(tm // CONV_BLOCK) for b in range(CONV_STRIDE)]


def _inproj_kernel(x_ref, nw_ref, w_ref, wdt_ref, qnw_ref, knw_ref, cw_ref, cb_ref,
                   dtb_ref, q_ref, k_ref, v_ref, z_ref, xbc_ref, dt_ref, cbuf, ybuf,
                   *, tm, tiles_per_seq):
    i = pl.program_id(0)
    h = _rms(x_ref[...], nw_ref[...]).astype(BF16)
    nchunk = XBC_WIDTH // LANES
    col_q, col_k, col_v, col_z, col_x = (
        0, ATT_WIDTH, 2 * ATT_WIDTH, 3 * ATT_WIDTH, 3 * ATT_WIDTH + SSM_WIDTH)

    @pl.when(i % tiles_per_seq == 0)
    def _():
        cbuf[:, 0:SUBLANES, :] = jnp.zeros((nchunk, SUBLANES, LANES), F32)

    raw = _dot(h, w_ref[:, col_x:col_x + XBC_WIDTH])
    for c in range(nchunk):
        cbuf[c, SUBLANES:SUBLANES + tm, :] = raw[:, _lane_chunk(c)]
    for c in range(nchunk):
        for row in _strided_rows(tm):
            acc = _conv_rows(cbuf, c, cw_ref, cb_ref, c, row, SUBLANES, SSM_CONV)
            ybuf[c, pl.ds(row, SUBLANES, stride=CONV_STRIDE), :] = _silu(acc)
        xbc_ref[:, _lane_chunk(c)] = ybuf[c].astype(BF16)
        cbuf[c, 0:SUBLANES, :] = cbuf[c, tm:tm + SUBLANES, :]

    lo_half = lax.broadcasted_iota(jnp.int32, (tm, ATT_V_DIM), 1) < ATT_HEAD_DIM

    def qk_norm(col, nw):
        p = _dot(h, w_ref[:, col:col + ATT_WIDTH])
        outs = []
        for hh in range(N_ATT_HEADS):
            ph = p[:, hh * ATT_V_DIM:(hh + 1) * ATT_V_DIM]
            sq = ph * ph
            s_lo = jnp.sum(jnp.where(lo_half, sq, 0.0), axis=-1, keepdims=True)
            s_hi = jnp.sum(jnp.where(lo_half, 0.0, sq), axis=-1, keepdims=True)
            ms = jnp.where(lo_half, s_lo, s_hi) * (1.0 / ATT_HEAD_DIM)
            outs.append(ph * lax.rsqrt(ms + RMS_EPS))
        return jnp.concatenate(outs, axis=1) * nw

    q_ref[...] = (qk_norm(col_q, qnw_ref[...]) * (ATT_SCALE * LOG2_E)).astype(BF16)
    k_ref[...] = qk_norm(col_k, knw_ref[...]).astype(BF16)
    v_ref[...] = _dot(h, w_ref[:, col_v:col_v + ATT_WIDTH]).astype(BF16)
    z_ref[...] = _silu(_dot(h, w_ref[:, col_z:col_z + SSM_WIDTH])).astype(BF16)
    dt_ref[...] = _softplus(_dot(h, wdt_ref[...]) + dtb_ref[...])


def _inproj(x2d, nw, w, wdt, qnw, knw, cw, cb, dtb, *, seq):
    m = x2d.shape[0]
    tm = ROW_TILE
    row = lambda c: pl.BlockSpec((tm, c), lambda i: (i, 0))
    ins = [x2d, nw, w, wdt, qnw, knw, cw, cb, dtb]
    in_specs = [row(D_MODEL)] + [_const_spec(a.shape) for a in ins[1:]]
    out_shape = [jax.ShapeDtypeStruct((m, ATT_WIDTH), BF16)] * 3 + [
        jax.ShapeDtypeStruct((m, SSM_WIDTH), BF16),
        jax.ShapeDtypeStruct((m, XBC_WIDTH), BF16),
        jax.ShapeDtypeStruct((m, LANES), F32)]
    out_specs = [row(ATT_WIDTH)] * 3 + [row(SSM_WIDTH), row(XBC_WIDTH), row(LANES)]
    nchunk = XBC_WIDTH // LANES
    return pl.pallas_call(
        functools.partial(_inproj_kernel, tm=tm, tiles_per_seq=seq // tm),
        grid=(m // tm,), in_specs=in_specs, out_specs=out_specs, out_shape=out_shape,
        scratch_shapes=[pltpu.VMEM((nchunk, tm + SUBLANES, LANES), F32),
                        pltpu.VMEM((nchunk, tm, LANES), F32)],
        compiler_params=_params(("arbitrary",)), name="inproj")(*ins)


def _attn_kernel(q_ref, k_ref, v_ref, lq1_ref, lk1_ref, lq2_ref, lk2_ref, sw_ref, o_ref,
                 vext, mst, acc, *, t, lam_init):
    i = pl.program_id(1)
    nh = N_ATT_HEADS
    head = lambda hh: slice(hh * ATT_V_DIM, (hh + 1) * ATT_V_DIM)

    @pl.when(i == 0)
    def _():
        for hh in range(nh):
            vext[hh, :, 0:ATT_V_DIM] = v_ref[0, :, head(hh)]
            vext[hh, :, ATT_V_DIM:] = jnp.ones((vext.shape[1], ATT_V_DIM), BF16)

    lane = lax.broadcasted_iota(jnp.int32, (t, ATT_V_DIM), 1)
    qs = []
    for hh in range(nh):
        q = q_ref[0, :, head(hh)]
        zero = jnp.zeros_like(q)
        qs.append(jnp.where(lane < ATT_HEAD_DIM, q, zero))
        qs.append(jnp.where(lane >= ATT_HEAD_DIM, q, zero))

    mst[...] = jnp.full(mst.shape, -1e30, F32)
    acc[...] = jnp.zeros(acc.shape, F32)

    def update(s, vb, idx):
        m_old = mst[idx]
        m_new = jnp.maximum(m_old, jnp.max(s, axis=-1, keepdims=True))
        alpha = jnp.exp2(m_old - m_new)
        p = jnp.exp2(s - jnp.tile(m_new, (1, t // LANES)))
        acc[idx] = jnp.tile(alpha, (1, 2)) * acc[idx] + _dot(p.astype(BF16), vb)
        mst[idx] = m_new

    def block(j, mask):
        r0 = pl.multiple_of(j * t, t)
        for hh in range(nh):
            kb = k_ref[0, pl.ds(r0, t), head(hh)]
            vb = vext[hh, pl.ds(r0, t), :]
            for c in range(2):
                s = _dot_nt(qs[2 * hh + c], kb)
                if mask is not None:
                    s = jnp.where(mask, s, -jnp.inf)
                update(s, vb, 2 * hh + c)

    def body(j, carry):
        block(j, None)
        return carry

    lax.fori_loop(0, i, body, 0)
    rr = lax.broadcasted_iota(jnp.int32, (t, t), 0)
    cc = lax.broadcasted_iota(jnp.int32, (t, t), 1)
    block(i, rr >= cc)

    lam = (jnp.exp(jnp.sum(lq1_ref[...] * lk1_ref[...], axis=-1, keepdims=True))
           - jnp.exp(jnp.sum(lq2_ref[...] * lk2_ref[...], axis=-1, keepdims=True)) + lam_init)
    for hh in range(nh):
        a1 = acc[2 * hh]
        a2 = acc[2 * hh + 1]
        o = (a1[:, 0:ATT_V_DIM] / a1[:, ATT_V_DIM:]
             - lam * (a2[:, 0:ATT_V_DIM] / a2[:, ATT_V_DIM:]))
        o_ref[0, :, head(hh)] = (_rms(o, sw_ref[...]) * (1.0 - lam_init)).astype(BF16)


def _attention(q, k, v, lq1, lk1, lq2, lk2, sw, *, lam_init):
    b, s, _ = q.shape
    t = ATT_TILE
    qspec = pl.BlockSpec((1, t, ATT_WIDTH), lambda bi, i: (bi, i, 0))
    kvspec = pl.BlockSpec((1, s, ATT_WIDTH), lambda bi, i: (bi, 0, 0))
    small = [lq1, lk1, lq2, lk2, sw]
    return pl.pallas_call(
        functools.partial(_attn_kernel, t=t, lam_init=lam_init),
        grid=(b, s // t),
        in_specs=[qspec, kvspec, kvspec] + [_const_spec(a.shape) for a in small],
        out_specs=qspec, out_shape=jax.ShapeDtypeStruct((b, s, ATT_WIDTH), BF16),
        scratch_shapes=[pltpu.VMEM((N_ATT_HEADS, s, 2 * ATT_V_DIM), BF16),
                        pltpu.VMEM((2 * N_ATT_HEADS, t, LANES), F32),
                        pltpu.VMEM((2 * N_ATT_HEADS, t, 2 * ATT_V_DIM), F32)],
        compiler_params=_params(("arbitrary", "arbitrary")),
        name="diff_attn")(q, k, v, *small)


def _expand_heads(v, sel):
    hi, lo, _ = _split3(v)
    return _dot(hi, sel) + _dot(lo, sel)


def _ssd_kernel(xs_ref, b_ref, c_ref, dt_ref, z_ref, alog_ref, dexp_ref, nw_ref, y_ref, state,
                *, tc, tiles_per_seq):
    i = pl.program_id(0)
    t = SSM_CHUNK
    gw = SSM_WIDTH // SSM_GROUPS
    hpg = SSM_HEADS // SSM_GROUPS

    @pl.when(i % tiles_per_seq == 0)
    def _():
        state[...] = jnp.zeros(state.shape, F32)

    a_head = -jnp.exp(alog_ref[...])
    rr = lax.broadcasted_iota(jnp.int32, (t, t), 0)
    cc = lax.broadcasted_iota(jnp.int32, (t, t), 1)
    tril = rr >= cc
    tri = tril.astype(BF16)
    lo_half = cc < SSM_HEAD_DIM
    sel = (lax.broadcasted_iota(jnp.int32, (LANES, SSM_WIDTH), 1) // SSM_HEAD_DIM
           == lax.broadcasted_iota(jnp.int32, (LANES, SSM_WIDTH), 0)).astype(BF16)

    def chunk(ci):
        rows = slice(ci * t, (ci + 1) * t)
        xs = xs_ref[rows, :].astype(F32)
        dt = dt_ref[rows, :]
        a = dt * a_head
        p0, p1, p2 = _split3(a)
        acum = _dot(tri, p0) + _dot(tri, p1) + _dot(tri, p2)
        acum_t = acum.T
        dtx = _expand_heads(dt, sel)
        acum_x = _expand_heads(acum, sel)
        alast_x = acum_x[t - 1:t, :]
        ea = jnp.exp(acum_x)
        dec = jnp.exp(alast_x - acum_x)
        elast = jnp.exp(alast_x)
        x = xs * dtx
        xb = x.astype(BF16)
        xdec = (x * dec).astype(BF16)
        ys = []
        for g in range(SSM_GROUPS):
            gs = slice(g * gw, (g + 1) * gw)
            bg = b_ref[rows, g * SSM_STATE:(g + 1) * SSM_STATE]
            cg = c_ref[rows, g * SSM_STATE:(g + 1) * SSM_STATE]
            cb = _dot_nt(cg, bg)
            st = state[g]
            y_g = _dot(cg, st.astype(BF16)) * ea[:, gs]
            pairs = []
            for pr in range(hpg // 2):
                xp = xb[:, g * gw + pr * LANES:g * gw + (pr + 1) * LANES]
                res = []
                for hh in range(2):
                    hd = g * hpg + 2 * pr + hh
                    diff = acum[:, hd:hd + 1] - acum_t[hd:hd + 1, :]
                    lm = jnp.exp(jnp.where(tril, diff, -jnp.inf))
                    res.append(_dot((cb * lm).astype(BF16), xp))
                pairs.append(jnp.where(lo_half, res[0], res[1]))
            ys.append(y_g + jnp.concatenate(pairs, axis=1))
            state[g] = st * elast[:, gs] + _dot_tn(bg, xdec[:, gs])
        y = jnp.concatenate(ys, axis=1) + xs * dexp_ref[...]
        y = y * z_ref[rows, :].astype(F32)
        outs = []
        for g in range(SSM_GROUPS):
            gs = slice(g * gw, (g + 1) * gw)
            outs.append(_rms(y[:, gs], nw_ref[:, gs]))
        y_ref[rows, :] = jnp.concatenate(outs, axis=1).astype(BF16)

    for ci in range(tc // t):
        chunk(ci)


def _ssd(xbc, dt, zs, alog, dexp, nw, *, seq):
    m = xbc.shape[0]
    tc = SSD_TILE
    bc_w = SSM_GROUPS * SSM_STATE
    ins = [xbc, xbc, xbc, dt, zs, alog, dexp, nw]
    in_specs = [pl.BlockSpec((tc, SSM_WIDTH), lambda i: (i, 0)),
                pl.BlockSpec((tc, bc_w), lambda i: (i, SSM_WIDTH // bc_w)),
                pl.BlockSpec((tc, bc_w), lambda i: (i, SSM_WIDTH // bc_w + 1)),
                pl.BlockSpec((tc, LANES), lambda i: (i, 0)),
                pl.BlockSpec((tc, SSM_WIDTH), lambda i: (i, 0))] + [
                    _const_spec(a.shape) for a in ins[5:]]
    return pl.pallas_call(
        functools.partial(_ssd_kernel, tc=tc, tiles_per_seq=seq // tc),
        grid=(m // tc,), in_specs=in_specs,
        out_specs=pl.BlockSpec((tc, SSM_WIDTH), lambda i: (i, 0)),
        out_shape=jax.ShapeDtypeStruct((m, SSM_WIDTH), BF16),
        scratch_shapes=[pltpu.VMEM((SSM_GROUPS, SSM_STATE, SSM_WIDTH // SSM_GROUPS), F32)],
        compiler_params=_params(("arbitrary",)), name="ssd")(*ins)


def _ffn_body(x, nw_ref, up_ref, cw_ref, cb_ref, down_ref,
              o_ref, gbuf, vbuf, abuf, act, *, tm, tiles_per_seq):
    i = pl.program_id(0)
    h = _rms(x, nw_ref[...]).astype(BF16)
    nchunk = FFN_DIM // LANES
    per_dot = MXU_DIM // LANES

    @pl.when(i % tiles_per_seq == 0)
    def _():
        gbuf[:, 0:SUBLANES, :] = jnp.zeros((nchunk, SUBLANES, LANES), F32)
        vbuf[:, 0:SUBLANES, :] = jnp.zeros((nchunk, SUBLANES, LANES), F32)

    for d in range(FFN_DIM // MXU_DIM):
        c0 = d * MXU_DIM
        rg = _dot(h, up_ref[:, c0:c0 + MXU_DIM])
        rv = _dot(h, up_ref[:, FFN_DIM + c0:FFN_DIM + c0 + MXU_DIM])
        for u in range(per_dot):
            c = d * per_dot + u
            gbuf[c, SUBLANES:SUBLANES + tm, :] = rg[:, _lane_chunk(u)]
            vbuf[c, SUBLANES:SUBLANES + tm, :] = rv[:, _lane_chunk(u)]
        for u in range(per_dot):
            c = d * per_dot + u
            for row in _strided_rows(tm):
                g = _conv_rows(gbuf, c, cw_ref, cb_ref, c, row, SUBLANES, FFN_CONV)
                v = _conv_rows(vbuf, c, cw_ref, cb_ref, nchunk + c, row, SUBLANES, FFN_CONV)
                abuf[c, pl.ds(row, SUBLANES, stride=CONV_STRIDE), :] = _silu(g) * v
            act[:, _lane_chunk(c)] = abuf[c].astype(BF16)
            gbuf[c, 0:SUBLANES, :] = gbuf[c, tm:tm + SUBLANES, :]
            vbuf[c, 0:SUBLANES, :] = vbuf[c, tm:tm + SUBLANES, :]

    o_ref[...] = x + _dot(act[...], down_ref[...])


def _ffn_kernel(x_ref, *rest, **kw):
    _ffn_body(x_ref[...], *rest, **kw)


def _mix_ffn_kernel(x_ref, att_ref, y_ref, wo_ref, *rest, **kw):
    x = (x_ref[...] + _dot(att_ref[...], wo_ref[0:ATT_WIDTH, :])
         + _dot(y_ref[...], wo_ref[ATT_WIDTH:, :]))
    _ffn_body(x, *rest, **kw)


def _ffn(x2d, nw, up, cw, cb, down, *, seq, mix=None):
    m = x2d.shape[0]
    tm = FFN_ROW_TILE
    nchunk = FFN_DIM // LANES
    row = lambda c: pl.BlockSpec((tm, c), lambda i: (i, 0))
    consts = [nw, up, cw, cb, down]
    if mix is None:
        body, ins, specs = _ffn_kernel, [x2d], [row(D_MODEL)]
    else:
        att, y, wo = mix
        body, ins = _mix_ffn_kernel, [x2d, att, y, wo]
        specs = [row(D_MODEL), row(ATT_WIDTH), row(SSM_WIDTH), _const_spec(wo.shape)]
    return pl.pallas_call(
        functools.partial(body, tm=tm, tiles_per_seq=seq // tm),
        grid=(m // tm,), in_specs=specs + [_const_spec(a.shape) for a in consts],
        out_specs=row(D_MODEL), out_shape=jax.ShapeDtypeStruct((m, D_MODEL), F32),
        scratch_shapes=[pltpu.VMEM((nchunk, tm + SUBLANES, LANES), F32),
                        pltpu.VMEM((nchunk, tm + SUBLANES, LANES), F32),
                        pltpu.VMEM((nchunk, tm, LANES), F32),
                        pltpu.VMEM((tm, FFN_DIM), BF16)],
        compiler_params=_params(("arbitrary",)), name="conv_ffn")(*ins, *consts)


def _conformer_kernel(x_ref, nw_ref, w1_ref, b1_ref, dww_ref, dwb_ref,
                      lnw_ref, lnb_ref, w2_ref, b2_ref, o_ref, cbuf, ybuf, *, tm, tiles_per_seq):
    i = pl.program_id(0)
    x = x_ref[...]
    h = _rms(x, nw_ref[...]).astype(BF16)
    nchunk = CONF_WIDTH // LANES

    @pl.when(i % tiles_per_seq == 0)
    def _():
        cbuf[:, 0:CONF_HEAD, :] = jnp.zeros((nchunk, CONF_HEAD, LANES), F32)

    ua = _dot(h, w1_ref[:, 0:CONF_WIDTH]) + b1_ref[:, 0:CONF_WIDTH]
    ub = _dot(h, w1_ref[:, CONF_WIDTH:]) + b1_ref[:, CONF_WIDTH:]
    glu = ua * _sigmoid(ub)
    for c in range(nchunk):
        cbuf[c, CONF_HEAD:CONF_HEAD + tm, :] = glu[:, _lane_chunk(c)]

    def conv_chunk(c, carry):
        for row in _strided_rows(tm):
            ybuf[c, pl.ds(row, SUBLANES, stride=CONV_STRIDE), :] = _conv_rows(
                cbuf, c, dww_ref, dwb_ref, c, row, CONF_HEAD, CONF_K)
        cbuf[c, 0:CONF_HEAD, :] = cbuf[c, tm:tm + CONF_HEAD, :]
        return carry

    lax.fori_loop(0, nchunk, conv_chunk, 0)

    u = jnp.concatenate([ybuf[c] for c in range(nchunk)], axis=1)
    mu = jnp.mean(u, axis=-1, keepdims=True)
    d = u - mu
    var = jnp.mean(d * d, axis=-1, keepdims=True)
    y = _silu(d * lax.rsqrt(var + LN_EPS) * lnw_ref[...] + lnb_ref[...]).astype(BF16)
    o_ref[...] = x + _dot(y, w2_ref[...]) + b2_ref[...]


def _conformer(x2d, nw, w1, b1, dww, dwb, lnw, lnb, w2, b2, *, seq):
    m = x2d.shape[0]
    tm = CONF_ROW_TILE
    nchunk = CONF_WIDTH // LANES
    row = pl.BlockSpec((tm, D_MODEL), lambda i: (i, 0))
    ins = [x2d, nw, w1, b1, dww, dwb, lnw, lnb, w2, b2]
    return pl.pallas_call(
        functools.partial(_conformer_kernel, tm=tm, tiles_per_seq=seq // tm),
        grid=(m // tm,), in_specs=[row] + [_const_spec(a.shape) for a in ins[1:]],
        out_specs=row, out_shape=jax.ShapeDtypeStruct((m, D_MODEL), F32),
        scratch_shapes=[pltpu.VMEM((nchunk, tm + CONF_HEAD, LANES), F32),
                        pltpu.VMEM((nchunk, tm, LANES), F32)],
        compiler_params=_params(("arbitrary",)), name="conformer")(*ins)


def _row(v):
    return v.reshape(1, -1).astype(F32)


def _chunk_taps(w):
    k, c = w.shape
    return w.astype(F32).reshape(k, c // LANES, LANES).transpose(1, 0, 2)


def _chunk_bias(b):
    return b.astype(F32).reshape(-1, 1, LANES)


def _pad_lanes(v, width=LANES):
    return jnp.pad(v, ((0, 0), (0, width - v.shape[1])))


def _mixer_branches(x2d, seq, lam_init, mix_norm_w, w_in, q_norm_w, k_norm_w, lq1, lk1, lq2, lk2,
                    attn_subln_w, conv_w, conv_b, dt_bias, a_log, d_skip, ssm_norm_w):
    m = x2d.shape[0]
    bsz = m // seq
    dt_col = 3 * ATT_WIDTH + SSM_WIDTH + XBC_WIDTH
    wdt = _pad_lanes(w_in[:, dt_col:].astype(BF16))
    reps = ATT_WIDTH // ATT_HEAD_DIM
    q, k, v, zs, xbc, dt = _inproj(
        x2d, _row(mix_norm_w), w_in.astype(BF16), wdt,
        _row(jnp.tile(q_norm_w, reps)), _row(jnp.tile(k_norm_w, reps)),
        _chunk_taps(conv_w), _chunk_bias(conv_b), _pad_lanes(_row(dt_bias)), seq=seq)
    shp = (bsz, seq, ATT_WIDTH)
    att = _attention(q.reshape(shp), k.reshape(shp), v.reshape(shp),
                     _row(lq1), _row(lk1), _row(lq2), _row(lk2), _row(attn_subln_w),
                     lam_init=lam_init).reshape(m, ATT_WIDTH)
    y = _ssd(xbc, dt, zs, _pad_lanes(_row(a_log)), _row(jnp.repeat(d_skip, SSM_HEAD_DIM)),
             _row(ssm_norm_w), seq=seq)
    return att, y


def _ffn_layer(x2d, seq, norm_w, up_w, conv_w, conv_b, down_w, mix=None):
    return _ffn(x2d, _row(norm_w), up_w.astype(BF16), _chunk_taps(conv_w), _chunk_bias(conv_b),
                down_w.astype(BF16), seq=seq, mix=mix)


def _conformer_layer(x2d, seq, norm_w, pw1_w, pw1_b, dw_w, dw_b, ln_w, ln_b, pw2_w, pw2_b):
    return _conformer(x2d, _row(norm_w), pw1_w.astype(BF16), _row(pw1_b),
                      _chunk_taps(dw_w), _chunk_bias(dw_b),
                      _row(ln_w), _row(ln_b), pw2_w.astype(BF16), _row(pw2_b), seq=seq)


def kernel(x, mix_norm_w, w_in, q_norm_w, k_norm_w, lambda_q1, lambda_k1, lambda_q2, lambda_k2,
           attn_subln_w, ssm_conv_w, ssm_conv_b, ssm_dt_bias, ssm_A_log, ssm_D, ssm_norm_w, w_out,
           conf_norm_w, conf_pw1_w, conf_pw1_b, conf_dw_w, conf_dw_b, conf_ln_w, conf_ln_b,
           conf_pw2_w, conf_pw2_b, ffn_norm_w, ffn_up_w, ffn_conv_w, ffn_conv_b, ffn_down_w):
    bsz, seq, d = x.shape
    depth = ffn_norm_w.shape[0]
    h = x.reshape(bsz * seq, d)
    for i in range(depth):
        mix = None
        if i % 2 == 0:
            e = i // 2
            lam_init = 0.8 - 0.6 * math.exp(-0.3 * i)
            att, y = _mixer_branches(
                h, seq, lam_init, mix_norm_w[e], w_in[e], q_norm_w[e], k_norm_w[e],
                lambda_q1[e], lambda_k1[e], lambda_q2[e], lambda_k2[e], attn_subln_w[e],
                ssm_conv_w[e], ssm_conv_b[e], ssm_dt_bias[e], ssm_A_log[e], ssm_D[e],
                ssm_norm_w[e])
            mix = (att, y, w_out[e].astype(BF16))
        else:
            o = i // 2
            h = _conformer_layer(h, seq, conf_norm_w[o], conf_pw1_w[o], conf_pw1_b[o],
                                 conf_dw_w[o], conf_dw_b[o], conf_ln_w[o], conf_ln_b[o],
                                 conf_pw2_w[o], conf_pw2_b[o])
        h = _ffn_layer(h, seq, ffn_norm_w[i], ffn_up_w[i], ffn_conv_w[i], ffn_conv_b[i],
                       ffn_down_w[i], mix=mix)
    return h.reshape(bsz, seq, d)
```

```python
import functools
import math

import jax
import jax.numpy as jnp
from jax import lax
from jax.experimental import pallas as pl
from jax.experimental.pallas import tpu as pltpu

F32 = jnp.float32
BF16 = jnp.bfloat16

D_MODEL = 1024
N_ATT_HEADS = 4
ATT_HEAD_DIM = 64
ATT_V_DIM = 2 * ATT_HEAD_DIM
ATT_WIDTH = N_ATT_HEADS * ATT_V_DIM
ATT_SCALE = ATT_HEAD_DIM ** -0.5
LOG2_E = math.log2(math.e)
SSM_WIDTH = D_MODEL - ATT_WIDTH
SSM_HEAD_DIM = 64
SSM_HEADS = SSM_WIDTH // SSM_HEAD_DIM
SSM_GROUPS = 2
SSM_STATE = 128
SSM_CONV = 4
SSM_CHUNK = 128
XBC_WIDTH = SSM_WIDTH + 2 * SSM_GROUPS * SSM_STATE
CONF_WIDTH = D_MODEL
CONF_K = 31
FFN_DIM = 2816
FFN_CONV = 3
RMS_EPS = 1e-6
LN_EPS = 1e-5

LANES = 128
SUBLANES = 8
MXU_DIM = 256
VMEM_LIMIT = 56 * 1024 * 1024

ROW_TILE = 512
ROW_GROUP = 256
FFN_ROW_TILE = 512
FFN_ROW_GROUP = 512
CONF_ROW_TILE = 512
CONF_ROW_GROUP = 256
ATT_TILE = 512
SSD_TILE = 512
CONV_STRIDE = 4
CONV_BLOCK = SUBLANES * CONV_STRIDE
CONF_HEAD = 32


def _dot(a, b):
    return jnp.dot(a, b, preferred_element_type=F32)


def _dot_nt(a, b):
    return lax.dot_general(a, b, (((1,), (1,)), ((), ())), preferred_element_type=F32)


def _dot_tn(a, b):
    return lax.dot_general(a, b, (((0,), (0,)), ((), ())), preferred_element_type=F32)


def _rms(x, w):
    ms = jnp.mean(x * x, axis=-1, keepdims=True)
    return x * lax.rsqrt(ms + RMS_EPS) * w


def _sigmoid(x):
    return 1.0 / (1.0 + jnp.exp(-x))


def _silu(x):
    return x * _sigmoid(x)


def _softplus(x):
    return jnp.maximum(x, 0.0) + jnp.log(1.0 + jnp.exp(-jnp.abs(x)))


def _split3(x):
    a = x.astype(BF16)
    r = x - a.astype(F32)
    b = r.astype(BF16)
    c = (r - b.astype(F32)).astype(BF16)
    return a, b, c


def _const_spec(shape):
    nd = len(shape)
    return pl.BlockSpec(shape, lambda *_: (0,) * nd, pipeline_mode=pl.Buffered(1))


def _layer_spec(shape, layer):
    nd = len(shape)
    return pl.BlockSpec((None,) + tuple(shape[1:]), lambda *_: (layer,) + (0,) * (nd - 1),
                        pipeline_mode=pl.Buffered(1))


def _params(sem):
    return pltpu.CompilerParams(dimension_semantics=sem, vmem_limit_bytes=VMEM_LIMIT)


def _lane_chunk(c):
    return slice(c * LANES, (c + 1) * LANES)


def _conv_rows(buf, c, w_ref, b_ref, wc, row, head, ntaps):
    first = head - (ntaps - 1) + row
    acc = b_ref[wc] + w_ref[wc, 0:1, :] * buf[c, pl.ds(first, SUBLANES, stride=CONV_STRIDE), :]
    for k in range(1, ntaps):
        acc = acc + w_ref[wc, k:k + 1, :] * buf[c, pl.ds(first + k, SUBLANES, stride=CONV_STRIDE), :]
    return acc


def _strided_rows(tm):
    return [blk * CONV_BLOCK + b for blk in range(tm // CONV_BLOCK) for b in range(CONV_STRIDE)]


def _inproj_kernel(x_ref, nw_ref, w_ref, wdt_ref, qnw_ref, knw_ref, cw_ref, cb_ref,
                   dtb_ref, q_ref, k_ref, v_ref, z_ref, xbc_ref, dt_ref, cbuf, ybuf,
                   *, tm, tiles_per_seq):
    i = pl.program_id(0)
    nchunk = XBC_WIDTH // LANES
    col_q, col_k, col_v, col_z, col_x = (
        0, ATT_WIDTH, 2 * ATT_WIDTH, 3 * ATT_WIDTH, 3 * ATT_WIDTH + SSM_WIDTH)
    groups = [(g * ROW_GROUP, (g + 1) * ROW_GROUP) for g in range(tm // ROW_GROUP)]

    @pl.when(i % tiles_per_seq == 0)
    def _():
        cbuf[:, 0:SUBLANES, :] = jnp.zeros((nchunk, SUBLANES, LANES), F32)

    hs = []
    for r0, r1 in groups:
        h = _rms(x_ref[r0:r1, :], nw_ref[...]).astype(BF16)
        raw = _dot(h, w_ref[:, col_x:col_x + XBC_WIDTH])
        for c in range(nchunk):
            cbuf[c, SUBLANES + r0:SUBLANES + r1, :] = raw[:, _lane_chunk(c)]
        hs.append(h)
    for c in range(nchunk):
        for row in _strided_rows(tm):
            acc = _conv_rows(cbuf, c, cw_ref, cb_ref, c, row, SUBLANES, SSM_CONV)
            ybuf[c, pl.ds(row, SUBLANES, stride=CONV_STRIDE), :] = _silu(acc)
        xbc_ref[:, _lane_chunk(c)] = ybuf[c].astype(BF16)
        cbuf[c, 0:SUBLANES, :] = cbuf[c, tm:tm + SUBLANES, :]

    lo_half = lax.broadcasted_iota(jnp.int32, (ROW_GROUP, ATT_V_DIM), 1) < ATT_HEAD_DIM

    def qk_norm(h, col, nw):
        p = _dot(h, w_ref[:, col:col + ATT_WIDTH])
        outs = []
        for hh in range(N_ATT_HEADS):
            ph = p[:, hh * ATT_V_DIM:(hh + 1) * ATT_V_DIM]
            sq = ph * ph
            s_lo = jnp.sum(jnp.where(lo_half, sq, 0.0), axis=-1, keepdims=True)
            s_hi = jnp.sum(jnp.where(lo_half, 0.0, sq), axis=-1, keepdims=True)
            ms = jnp.where(lo_half, s_lo, s_hi) * (1.0 / ATT_HEAD_DIM)
            outs.append(ph * lax.rsqrt(ms + RMS_EPS))
        return jnp.concatenate(outs, axis=1) * nw

    for (r0, r1), h in zip(groups, hs):
        q_ref[r0:r1, :] = (qk_norm(h, col_q, qnw_ref[...]) * (ATT_SCALE * LOG2_E)).astype(BF16)
        k_ref[r0:r1, :] = qk_norm(h, col_k, knw_ref[...]).astype(BF16)
        v_ref[r0:r1, :] = _dot(h, w_ref[:, col_v:col_v + ATT_WIDTH]).astype(BF16)
        z_ref[r0:r1, :] = _silu(_dot(h, w_ref[:, col_z:col_z + SSM_WIDTH])).astype(BF16)
        dt_ref[r0:r1, :] = _softplus(_dot(h, wdt_ref[...]) + dtb_ref[...])


def _inproj(x2d, nw, w, wdt, qnw, knw, cw, cb, dtb, *, seq):
    m = x2d.shape[0]
    tm = ROW_TILE
    row = lambda c: pl.BlockSpec((tm, c), lambda i: (i, 0))
    ins = [x2d, nw, w, wdt, qnw, knw, cw, cb, dtb]
    in_specs = [row(D_MODEL)] + [_const_spec(a.shape) for a in ins[1:]]
    out_shape = [jax.ShapeDtypeStruct((m, ATT_WIDTH), BF16)] * 3 + [
        jax.ShapeDtypeStruct((m, SSM_WIDTH), BF16),
        jax.ShapeDtypeStruct((m, XBC_WIDTH), BF16),
        jax.ShapeDtypeStruct((m, LANES), F32)]
    out_specs = [row(ATT_WIDTH)] * 3 + [row(SSM_WIDTH), row(XBC_WIDTH), row(LANES)]
    nchunk = XBC_WIDTH // LANES
    return pl.pallas_call(
        functools.partial(_inproj_kernel, tm=tm, tiles_per_seq=seq // tm),
        grid=(m // tm,), in_specs=in_specs, out_specs=out_specs, out_shape=out_shape,
        scratch_shapes=[pltpu.VMEM((nchunk, tm + SUBLANES, LANES), F32),
                        pltpu.VMEM((nchunk, tm, LANES), F32)],
        compiler_params=_params(("arbitrary",)), name="inproj")(*ins)


def _attn_kernel(q_ref, k_ref, v_ref, lq1_ref, lk1_ref, lq2_ref, lk2_ref, sw_ref, o_ref,
                 vext, mst, acc, *, t, lam_init):
    i = pl.program_id(1)
    nh = N_ATT_HEADS
    head = lambda hh: slice(hh * ATT_V_DIM, (hh + 1) * ATT_V_DIM)

    @pl.when(i == 0)
    def _():
        for hh in range(nh):
            vext[hh, :, 0:ATT_V_DIM] = v_ref[0, :, head(hh)]
            vext[hh, :, ATT_V_DIM:] = jnp.ones((vext.shape[1], ATT_V_DIM), BF16)

    lane = lax.broadcasted_iota(jnp.int32, (t, ATT_V_DIM), 1)
    qs = []
    for hh in range(nh):
        q = q_ref[0, :, head(hh)]
        zero = jnp.zeros_like(q)
        qs.append(jnp.where(lane < ATT_HEAD_DIM, q, zero))
        qs.append(jnp.where(lane >= ATT_HEAD_DIM, q, zero))

    mst[...] = jnp.full(mst.shape, -1e30, F32)
    acc[...] = jnp.zeros(acc.shape, F32)

    def update(s, vb, idx):
        m_old = mst[idx]
        m_new = jnp.maximum(m_old, jnp.max(s, axis=-1, keepdims=True))
        alpha = jnp.exp2(m_old - m_new)
        p = jnp.exp2(s - jnp.tile(m_new, (1, t // LANES)))
        acc[idx] = jnp.tile(alpha, (1, 2)) * acc[idx] + _dot(p.astype(BF16), vb)
        mst[idx] = m_new

    def block(j, mask):
        r0 = pl.multiple_of(j * t, t)
        for hh in range(nh):
            kb = k_ref[0, pl.ds(r0, t), head(hh)]
            vb = vext[hh, pl.ds(r0, t), :]
            for c in range(2):
                s = _dot_nt(qs[2 * hh + c], kb)
                if mask is not None:
                    s = jnp.where(mask, s, -jnp.inf)
                update(s, vb, 2 * hh + c)

    def body(j, carry):
        block(j, None)
        return carry

    lax.fori_loop(0, i, body, 0)
    rr = lax.broadcasted_iota(jnp.int32, (t, t), 0)
    cc = lax.broadcasted_iota(jnp.int32, (t, t), 1)
    block(i, rr >= cc)

    lam = (jnp.exp(jnp.sum(lq1_ref[...] * lk1_ref[...], axis=-1, keepdims=True))
           - jnp.exp(jnp.sum(lq2_ref[...] * lk2_ref[...], axis=-1, keepdims=True)) + lam_init)
    for hh in range(nh):
        a1 = acc[2 * hh]
        a2 = acc[2 * hh + 1]
        o = (a1[:, 0:ATT_V_DIM] / a1[:, ATT_V_DIM:]
             - lam * (a2[:, 0:ATT_V_DIM] / a2[:, ATT_V_DIM:]))
        o_ref[0, :, head(hh)] = (_rms(o, sw_ref[...]) * (1.0 - lam_init)).astype(BF16)


def _attention(q, k, v, lq1, lk1, lq2, lk2, sw, *, lam_init):
    b, s, _ = q.shape
    t = ATT_TILE
    qspec = pl.BlockSpec((1, t, ATT_WIDTH), lambda bi, i: (bi, i, 0))
    kvspec = pl.BlockSpec((1, s, ATT_WIDTH), lambda bi, i: (bi, 0, 0))
    small = [lq1, lk1, lq2, lk2, sw]
    return pl.pallas_call(
        functools.partial(_attn_kernel, t=t, lam_init=lam_init),
        grid=(b, s // t),
        in_specs=[qspec, kvspec, kvspec] + [_const_spec(a.shape) for a in small],
        out_specs=qspec, out_shape=jax.ShapeDtypeStruct((b, s, ATT_WIDTH), BF16),
        scratch_shapes=[pltpu.VMEM((N_ATT_HEADS, s, 2 * ATT_V_DIM), BF16),
                        pltpu.VMEM((2 * N_ATT_HEADS, t, LANES), F32),
                        pltpu.VMEM((2 * N_ATT_HEADS, t, 2 * ATT_V_DIM), F32)],
        compiler_params=_params(("arbitrary", "arbitrary")),
        name="diff_attn")(q, k, v, *small)


def _expand_heads(v, sel):
    hi, lo, _ = _split3(v)
    return _dot(hi, sel) + _dot(lo, sel)


def _ssd_kernel(xs_ref, b_ref, c_ref, dt_ref, z_ref, alog_ref, dexp_ref, nw_ref, y_ref, state,
                *, tc, tiles_per_seq):
    i = pl.program_id(0)
    t = SSM_CHUNK
    gw = SSM_WIDTH // SSM_GROUPS
    hpg = SSM_HEADS // SSM_GROUPS

    @pl.when(i % tiles_per_seq == 0)
    def _():
        state[...] = jnp.zeros(state.shape, F32)

    a_head = -jnp.exp(alog_ref[...])
    rr = lax.broadcasted_iota(jnp.int32, (t, t), 0)
    cc = lax.broadcasted_iota(jnp.int32, (t, t), 1)
    tril = rr >= cc
    tri = tril.astype(BF16)
    lo_half = cc < SSM_HEAD_DIM
    sel = (lax.broadcasted_iota(jnp.int32, (LANES, SSM_WIDTH), 1) // SSM_HEAD_DIM
           == lax.broadcasted_iota(jnp.int32, (LANES, SSM_WIDTH), 0)).astype(BF16)

    def chunk(ci):
        rows = slice(ci * t, (ci + 1) * t)
        xs = xs_ref[rows, :].astype(F32)
        dt = dt_ref[rows, :]
        a = dt * a_head
        p0, p1, p2 = _split3(a)
        acum = _dot(tri, p0) + _dot(tri, p1) + _dot(tri, p2)
        acum_t = acum.T
        dtx = _expand_heads(dt, sel)
        acum_x = _expand_heads(acum, sel)
        alast_x = acum_x[t - 1:t, :]
        ea = jnp.exp(acum_x)
        dec = jnp.exp(alast_x - acum_x)
        elast = jnp.exp(alast_x)
        x = xs * dtx
        xb = x.astype(BF16)
        xdec = (x * dec).astype(BF16)
        ys = []
        for g in range(SSM_GROUPS):
            gs = slice(g * gw, (g + 1) * gw)
            bg = b_ref[rows, g * SSM_STATE:(g + 1) * SSM_STATE]
            cg = c_ref[rows, g * SSM_STATE:(g + 1) * SSM_STATE]
            cb = _dot_nt(cg, bg)
            st = state[g]
            y_g = _dot(cg, st.astype(BF16)) * ea[:, gs]
            pairs = []
            for pr in range(hpg // 2):
                xp = xb[:, g * gw + pr * LANES:g * gw + (pr + 1) * LANES]
                res = []
                for hh in range(2):
                    hd = g * hpg + 2 * pr + hh
                    diff = acum[:, hd:hd + 1] - acum_t[hd:hd + 1, :]
                    lm = jnp.exp(jnp.where(tril, diff, -jnp.inf))
                    res.append(_dot((cb * lm).astype(BF16), xp))
                pairs.append(jnp.where(lo_half, res[0], res[1]))
            ys.append(y_g + jnp.concatenate(pairs, axis=1))
            state[g] = st * elast[:, gs] + _dot_tn(bg, xdec[:, gs])
        y = jnp.concatenate(ys, axis=1) + xs * dexp_ref[...]
        y = y * z_ref[rows, :].astype(F32)
        outs = []
        for g in range(SSM_GROUPS):
            gs = slice(g * gw, (g + 1) * gw)
            outs.append(_rms(y[:, gs], nw_ref[:, gs]))
        y_ref[rows, :] = jnp.concatenate(outs, axis=1).astype(BF16)

    for ci in range(tc // t):
        chunk(ci)


def _ssd(xbc, dt, zs, alog, dexp, nw, *, seq):
    m = xbc.shape[0]
    tc = SSD_TILE
    bc_w = SSM_GROUPS * SSM_STATE
    ins = [xbc, xbc, xbc, dt, zs, alog, dexp, nw]
    in_specs = [pl.BlockSpec((tc, SSM_WIDTH), lambda i: (i, 0)),
                pl.BlockSpec((tc, bc_w), lambda i: (i, SSM_WIDTH // bc_w)),
                pl.BlockSpec((tc, bc_w), lambda i: (i, SSM_WIDTH // bc_w + 1)),
                pl.BlockSpec((tc, LANES), lambda i: (i, 0)),
                pl.BlockSpec((tc, SSM_WIDTH), lambda i: (i, 0))] + [
                    _const_spec(a.shape) for a in ins[5:]]
    return pl.pallas_call(
        functools.partial(_ssd_kernel, tc=tc, tiles_per_seq=seq // tc),
        grid=(m // tc,), in_specs=in_specs,
        out_specs=pl.BlockSpec((tc, SSM_WIDTH), lambda i: (i, 0)),
        out_shape=jax.ShapeDtypeStruct((m, SSM_WIDTH), BF16),
        scratch_shapes=[pltpu.VMEM((SSM_GROUPS, SSM_STATE, SSM_WIDTH // SSM_GROUPS), F32)],
        compiler_params=_params(("arbitrary",)), name="ssd")(*ins)


def _ffn_body(xs, nw_ref, up_ref, cw_ref, cb_ref, down_ref,
              o_ref, gbuf, vbuf, abuf, act, *, tm, tiles_per_seq):
    i = pl.program_id(0)
    hs = [_rms(x, nw_ref[...]).astype(BF16) for x in xs]
    nchunk = FFN_DIM // LANES
    per_dot = MXU_DIM // LANES

    @pl.when(i % tiles_per_seq == 0)
    def _():
        gbuf[:, 0:SUBLANES, :] = jnp.zeros((nchunk, SUBLANES, LANES), F32)
        vbuf[:, 0:SUBLANES, :] = jnp.zeros((nchunk, SUBLANES, LANES), F32)

    for d in range(FFN_DIM // MXU_DIM):
        c0 = d * MXU_DIM
        for g, h in enumerate(hs):
            r0 = SUBLANES + g * FFN_ROW_GROUP
            rg = _dot(h, up_ref[:, c0:c0 + MXU_DIM])
            rv = _dot(h, up_ref[:, FFN_DIM + c0:FFN_DIM + c0 + MXU_DIM])
            for u in range(per_dot):
                c = d * per_dot + u
                gbuf[c, r0:r0 + FFN_ROW_GROUP, :] = rg[:, _lane_chunk(u)]
                vbuf[c, r0:r0 + FFN_ROW_GROUP, :] = rv[:, _lane_chunk(u)]
        for u in range(per_dot):
            c = d * per_dot + u
            for row in _strided_rows(tm):
                g = _conv_rows(gbuf, c, cw_ref, cb_ref, c, row, SUBLANES, FFN_CONV)
                v = _conv_rows(vbuf, c, cw_ref, cb_ref, nchunk + c, row, SUBLANES, FFN_CONV)
                abuf[c, pl.ds(row, SUBLANES, stride=CONV_STRIDE), :] = _silu(g) * v
            act[:, _lane_chunk(c)] = abuf[c].astype(BF16)
            gbuf[c, 0:SUBLANES, :] = gbuf[c, tm:tm + SUBLANES, :]
            vbuf[c, 0:SUBLANES, :] = vbuf[c, tm:tm + SUBLANES, :]

    for g, x in enumerate(xs):
        rows = slice(g * FFN_ROW_GROUP, (g + 1) * FFN_ROW_GROUP)
        o_ref[rows, :] = x + _dot(act[rows, :], down_ref[...])


def _row_groups(tm):
    return [slice(g * FFN_ROW_GROUP, (g + 1) * FFN_ROW_GROUP) for g in range(tm // FFN_ROW_GROUP)]


def _ffn_kernel(x_ref, *rest, tm, **kw):
    _ffn_body([x_ref[rows, :] for rows in _row_groups(tm)], *rest, tm=tm, **kw)


def _mix_ffn_kernel(x_ref, att_ref, y_ref, wo_ref, *rest, tm, **kw):
    xs = [x_ref[rows, :] + _dot(att_ref[rows, :], wo_ref[0:ATT_WIDTH, :])
          + _dot(y_ref[rows, :], wo_ref[ATT_WIDTH:, :]) for rows in _row_groups(tm)]
    _ffn_body(xs, *rest, tm=tm, **kw)


def _ffn(x2d, nw, up_all, cw, cb, down_all, *, seq, layer, mix=None):
    m = x2d.shape[0]
    tm = FFN_ROW_TILE
    nchunk = FFN_DIM // LANES
    row = lambda c: pl.BlockSpec((tm, c), lambda i: (i, 0))
    consts = [nw, up_all, cw, cb, down_all]
    const_specs = [_const_spec(nw.shape), _layer_spec(up_all.shape, layer), _const_spec(cw.shape),
                   _const_spec(cb.shape), _layer_spec(down_all.shape, layer)]
    if mix is None:
        body, ins, specs = _ffn_kernel, [x2d], [row(D_MODEL)]
    else:
        att, y, wo = mix
        body, ins = _mix_ffn_kernel, [x2d, att, y, wo]
        specs = [row(D_MODEL), row(ATT_WIDTH), row(SSM_WIDTH), _const_spec(wo.shape)]
    return pl.pallas_call(
        functools.partial(body, tm=tm, tiles_per_seq=seq // tm),
        grid=(m // tm,), in_specs=specs + const_specs,
        out_specs=row(D_MODEL), out_shape=jax.ShapeDtypeStruct((m, D_MODEL), F32),
        scratch_shapes=[pltpu.VMEM((nchunk, tm + SUBLANES, LANES), F32),
                        pltpu.VMEM((nchunk, tm + SUBLANES, LANES), F32),
                        pltpu.VMEM((nchunk, tm, LANES), F32),
                        pltpu.VMEM((tm, FFN_DIM), BF16)],
        compiler_params=_params(("arbitrary",)), name="conv_ffn")(*ins, *consts)


def _conformer_kernel(x_ref, nw_ref, w1_ref, b1_ref, dww_ref, dwb_ref,
                      lnw_ref, lnb_ref, w2_ref, b2_ref, o_ref, cbuf, ybuf, *, tm, tiles_per_seq):
    i = pl.program_id(0)
    nchunk = CONF_WIDTH // LANES
    groups = [(g * CONF_ROW_GROUP, (g + 1) * CONF_ROW_GROUP) for g in range(tm // CONF_ROW_GROUP)]

    @pl.when(i % tiles_per_seq == 0)
    def _():
        cbuf[:, 0:CONF_HEAD, :] = jnp.zeros((nchunk, CONF_HEAD, LANES), F32)

    for r0, r1 in groups:
        h = _rms(x_ref[r0:r1, :], nw_ref[...]).astype(BF16)
        ua = _dot(h, w1_ref[:, 0:CONF_WIDTH]) + b1_ref[:, 0:CONF_WIDTH]
        ub = _dot(h, w1_ref[:, CONF_WIDTH:]) + b1_ref[:, CONF_WIDTH:]
        glu = ua * _sigmoid(ub)
        for c in range(nchunk):
            cbuf[c, CONF_HEAD + r0:CONF_HEAD + r1, :] = glu[:, _lane_chunk(c)]

    def conv_chunk(c, carry):
        for row in _strided_rows(tm):
            ybuf[c, pl.ds(row, SUBLANES, stride=CONV_STRIDE), :] = _conv_rows(
                cbuf, c, dww_ref, dwb_ref, c, row, CONF_HEAD, CONF_K)
        cbuf[c, 0:CONF_HEAD, :] = cbuf[c, tm:tm + CONF_HEAD, :]
        return carry

    lax.fori_loop(0, nchunk, conv_chunk, 0)

    for r0, r1 in groups:
        u = jnp.concatenate([ybuf[c, r0:r1, :] for c in range(nchunk)], axis=1)
        mu = jnp.mean(u, axis=-1, keepdims=True)
        d = u - mu
        var = jnp.mean(d * d, axis=-1, keepdims=True)
        y = _silu(d * lax.rsqrt(var + LN_EPS) * lnw_ref[...] + lnb_ref[...]).astype(BF16)
        o_ref[r0:r1, :] = x_ref[r0:r1, :] + _dot(y, w2_ref[...]) + b2_ref[...]


def _conformer(x2d, nw, w1, b1, dww, dwb, lnw, lnb, w2, b2, *, seq):
    m = x2d.shape[0]
    tm = CONF_ROW_TILE
    nchunk = CONF_WIDTH // LANES
    row = pl.BlockSpec((tm, D_MODEL), lambda i: (i, 0))
    ins = [x2d, nw, w1, b1, dww, dwb, lnw, lnb, w2, b2]
    return pl.pallas_call(
        functools.partial(_conformer_kernel, tm=tm, tiles_per_seq=seq // tm),
        grid=(m // tm,), in_specs=[row] + [_const_spec(a.shape) for a in ins[1:]],
        out_specs=row, out_shape=jax.ShapeDtypeStruct((m, D_MODEL), F32),
        scratch_shapes=[pltpu.VMEM((nchunk, tm + CONF_HEAD, LANES), F32),
                        pltpu.VMEM((nchunk, tm, LANES), F32)],
        compiler_params=_params(("arbitrary",)), name="conformer")(*ins)


def _row(v):
    return v.reshape(1, -1).astype(F32)


def _chunk_taps(w):
    k, c = w.shape
    return w.astype(F32).reshape(k, c // LANES, LANES).transpose(1, 0, 2)


def _chunk_bias(b):
    return b.astype(F32).reshape(-1, 1, LANES)


def _pad_lanes(v, width=LANES):
    return jnp.pad(v, ((0, 0), (0, width - v.shape[1])))


def _mixer_branches(x2d, seq, lam_init, mix_norm_w, w_in, q_norm_w, k_norm_w, lq1, lk1, lq2, lk2,
                    attn_subln_w, conv_w, conv_b, dt_bias, a_log, d_skip, ssm_norm_w):
    m = x2d.shape[0]
    bsz = m // seq
    dt_col = 3 * ATT_WIDTH + SSM_WIDTH + XBC_WIDTH
    wdt = _pad_lanes(w_in[:, dt_col:].astype(BF16))
    reps = ATT_WIDTH // ATT_HEAD_DIM
    q, k, v, zs, xbc, dt = _inproj(
        x2d, _row(mix_norm_w), w_in.astype(BF16), wdt,
        _row(jnp.tile(q_norm_w, reps)), _row(jnp.tile(k_norm_w, reps)),
        _chunk_taps(conv_w), _chunk_bias(conv_b), _pad_lanes(_row(dt_bias)), seq=seq)
    shp = (bsz, seq, ATT_WIDTH)
    att = _attention(q.reshape(shp), k.reshape(shp), v.reshape(shp),
                     _row(lq1), _row(lk1), _row(lq2), _row(lk2), _row(attn_subln_w),
                     lam_init=lam_init).reshape(m, ATT_WIDTH)
    y = _ssd(xbc, dt, zs, _pad_lanes(_row(a_log)), _row(jnp.repeat(d_skip, SSM_HEAD_DIM)),
             _row(ssm_norm_w), seq=seq)
    return att, y


def _ffn_layer(x2d, seq, layer, norm_w, up_all, conv_w, conv_b, down_all, mix=None):
    return _ffn(x2d, _row(norm_w), up_all, _chunk_taps(conv_w), _chunk_bias(conv_b),
                down_all, seq=seq, layer=layer, mix=mix)


def _conformer_layer(x2d, seq, norm_w, pw1_w, pw1_b, dw_w, dw_b, ln_w, ln_b, pw2_w, pw2_b):
    return _conformer(x2d, _row(norm_w), pw1_w.astype(BF16), _row(pw1_b),
                      _chunk_taps(dw_w), _chunk_bias(dw_b),
                      _row(ln_w), _row(ln_b), pw2_w.astype(BF16), _row(pw2_b), seq=seq)


def kernel(x, mix_norm_w, w_in, q_norm_w, k_norm_w, lambda_q1, lambda_k1, lambda_q2, lambda_k2,
           attn_subln_w, ssm_conv_w, ssm_conv_b, ssm_dt_bias, ssm_A_log, ssm_D, ssm_norm_w, w_out,
           conf_norm_w, conf_pw1_w, conf_pw1_b, conf_dw_w, conf_dw_b, conf_ln_w, conf_ln_b,
           conf_pw2_w, conf_pw2_b, ffn_norm_w, ffn_up_w, ffn_conv_w, ffn_conv_b, ffn_down_w):
    bsz, seq, d = x.shape
    depth = ffn_norm_w.shape[0]
    up_all = ffn_up_w.astype(BF16)
    down_all = ffn_down_w.astype(BF16)
    h = x.reshape(bsz * seq, d)
    for i in range(depth):
        mix = None
        if i % 2 == 0:
            e = i // 2
            lam_init = 0.8 - 0.6 * math.exp(-0.3 * i)
            att, y = _mixer_branches(
                h, seq, lam_init, mix_norm_w[e], w_in[e], q_norm_w[e], k_norm_w[e],
                lambda_q1[e], lambda_k1[e], lambda_q2[e], lambda_k2[e], attn_subln_w[e],
                ssm_conv_w[e], ssm_conv_b[e], ssm_dt_bias[e], ssm_A_log[e], ssm_D[e],
                ssm_norm_w[e])
            mix = (att, y, w_out[e].astype(BF16))
        else:
            o = i // 2
            h = _conformer_layer(h, seq, conf_norm_w[o], conf_pw1_w[o], conf_pw1_b[o],
                                 conf_dw_w[o], conf_dw_b[o], conf_ln_w[o], conf_ln_b[o],
                                 conf_pw2_w[o], conf_pw2_b[o])
        h = _ffn_layer(h, seq, i, ffn_norm_w[i], up_all, ffn_conv_w[i], ffn_conv_b[i],
                       down_all, mix=mix)
    return h.reshape(bsz, seq, d)
```

```python
import functools
import math

import jax
import jax.numpy as jnp
from jax import lax
from jax.experimental import pallas as pl
from jax.experimental.pallas import tpu as pltpu

F32 = jnp.float32
BF16 = jnp.bfloat16

D_MODEL = 1024
N_ATT_HEADS = 4
ATT_HEAD_DIM = 64
ATT_V_DIM = 2 * ATT_HEAD_DIM
ATT_WIDTH = N_ATT_HEADS * ATT_V_DIM
ATT_SCALE = ATT_HEAD_DIM ** -0.5
LOG2_E = math.log2(math.e)
SSM_WIDTH = D_MODEL - ATT_WIDTH
SSM_HEAD_DIM = 64
SSM_HEADS = SSM_WIDTH // SSM_HEAD_DIM
SSM_GROUPS = 2
SSM_STATE = 128
SSM_CONV = 4
SSM_CHUNK = 128
XBC_WIDTH = SSM_WIDTH + 2 * SSM_GROUPS * SSM_STATE
CONF_WIDTH = D_MODEL
CONF_K = 31
FFN_DIM = 2816
FFN_CONV = 3
RMS_EPS = 1e-6
LN_EPS = 1e-5

LANES = 128
SUBLANES = 8
MXU_DIM = 256
VMEM_LIMIT = 56 * 1024 * 1024

ROW_TILE = 512
ROW_GROUP = 256
FFN_ROW_TILE = 512
FFN_ROW_GROUP = 512
CONF_ROW_TILE = 512
CONF_ROW_GROUP = 256
ATT_TILE = 512
SSD_TILE = 1024
CONV_STRIDE = 4
CONV_BLOCK = SUBLANES * CONV_STRIDE
CONF_HEAD = 32


def _dot(a, b):
    return jnp.dot(a, b, preferred_element_type=F32)


def _dot_nt(a, b):
    return lax.dot_general(a, b, (((1,), (1,)), ((), ())), preferred_element_type=F32)


def _dot_tn(a, b):
    return lax.dot_general(a, b, (((0,), (0,)), ((), ())), preferred_element_type=F32)


def _rms(x, w):
    ms = jnp.mean(x * x, axis=-1, keepdims=True)
    return x * lax.rsqrt(ms + RMS_EPS) * w


def _sigmoid(x):
    return 1.0 / (1.0 + jnp.exp(-x))


def _silu(x):
    return x * _sigmoid(x)


def _softplus(x):
    return jnp.maximum(x, 0.0) + jnp.log(1.0 + jnp.exp(-jnp.abs(x)))


def _split3(x):
    a = x.astype(BF16)
    r = x - a.astype(F32)
    b = r.astype(BF16)
    c = (r - b.astype(F32)).astype(BF16)
    return a, b, c


def _const_spec(shape):
    nd = len(shape)
    return pl.BlockSpec(shape, lambda *_: (0,) * nd, pipeline_mode=pl.Buffered(1))


def _layer_spec(shape, layer):
    nd = len(shape)
    return pl.BlockSpec((None,) + tuple(shape[1:]), lambda *_: (layer,) + (0,) * (nd - 1),
                        pipeline_mode=pl.Buffered(1))


def _params(sem):
    return pltpu.CompilerParams(dimension_semantics=sem, vmem_limit_bytes=VMEM_LIMIT)


def _lane_chunk(c):
    return slice(c * LANES, (c + 1) * LANES)


def _conv_rows(buf, c, w_ref, b_ref, wc, row, head, ntaps):
    first = head - (ntaps - 1) + row
    acc = b_ref[wc] + w_ref[wc, 0:1, :] * buf[c, pl.ds(first, SUBLANES, stride=CONV_STRIDE), :]
    for k in range(1, ntaps):
        acc = acc + w_ref[wc, k:k + 1, :] * buf[c, pl.ds(first + k, SUBLANES, stride=CONV_STRIDE), :]
    return acc


def _strided_rows(tm):
    return [blk * CONV_BLOCK + b for blk in range(tm // CONV_BLOCK) for b in range(CONV_STRIDE)]


def _inproj_kernel(x_ref, nw_ref, w_ref, wdt_ref, qnw_ref, knw_ref, cw_ref, cb_ref,
                   dtb_ref, q_ref, k_ref, v_ref, z_ref, xbc_ref, dt_ref, cbuf, ybuf,
                   *, tm, tiles_per_seq):
    i = pl.program_id(0)
    nchunk = XBC_WIDTH // LANES
    col_q, col_k, col_v, col_z, col_x = (
        0, ATT_WIDTH, 2 * ATT_WIDTH, 3 * ATT_WIDTH, 3 * ATT_WIDTH + SSM_WIDTH)
    groups = [(g * ROW_GROUP, (g + 1) * ROW_GROUP) for g in range(tm // ROW_GROUP)]

    @pl.when(i % tiles_per_seq == 0)
    def _():
        cbuf[:, 0:SUBLANES, :] = jnp.zeros((nchunk, SUBLANES, LANES), F32)

    hs = []
    for r0, r1 in groups:
        h = _rms(x_ref[r0:r1, :], nw_ref[...]).astype(BF16)
        raw = _dot(h, w_ref[:, col_x:col_x + XBC_WIDTH])
        for c in range(nchunk):
            cbuf[c, SUBLANES + r0:SUBLANES + r1, :] = raw[:, _lane_chunk(c)]
        hs.append(h)
    for c in range(nchunk):
        for row in _strided_rows(tm):
            acc = _conv_rows(cbuf, c, cw_ref, cb_ref, c, row, SUBLANES, SSM_CONV)
            ybuf[c, pl.ds(row, SUBLANES, stride=CONV_STRIDE), :] = _silu(acc)
        xbc_ref[:, _lane_chunk(c)] = ybuf[c].astype(BF16)
        cbuf[c, 0:SUBLANES, :] = cbuf[c, tm:tm + SUBLANES, :]

    lo_half = lax.broadcasted_iota(jnp.int32, (ROW_GROUP, ATT_V_DIM), 1) < ATT_HEAD_DIM

    def qk_norm(h, col, nw):
        p = _dot(h, w_ref[:, col:col + ATT_WIDTH])
        outs = []
        for hh in range(N_ATT_HEADS):
            ph = p[:, hh * ATT_V_DIM:(hh + 1) * ATT_V_DIM]
            sq = ph * ph
            s_lo = jnp.sum(jnp.where(lo_half, sq, 0.0), axis=-1, keepdims=True)
            s_hi = jnp.sum(jnp.where(lo_half, 0.0, sq), axis=-1, keepdims=True)
            ms = jnp.where(lo_half, s_lo, s_hi) * (1.0 / ATT_HEAD_DIM)
            outs.append(ph * lax.rsqrt(ms + RMS_EPS))
        return jnp.concatenate(outs, axis=1) * nw

    for (r0, r1), h in zip(groups, hs):
        q_ref[r0:r1, :] = (qk_norm(h, col_q, qnw_ref[...]) * (ATT_SCALE * LOG2_E)).astype(BF16)
        k_ref[r0:r1, :] = qk_norm(h, col_k, knw_ref[...]).astype(BF16)
        v_ref[r0:r1, :] = _dot(h, w_ref[:, col_v:col_v + ATT_WIDTH]).astype(BF16)
        z_ref[r0:r1, :] = _silu(_dot(h, w_ref[:, col_z:col_z + SSM_WIDTH])).astype(BF16)
        dt_ref[r0:r1, :] = _softplus(_dot(h, wdt_ref[...]) + dtb_ref[...])


def _inproj(x2d, nw, w, wdt, qnw, knw, cw, cb, dtb, *, seq):
    m = x2d.shape[0]
    tm = ROW_TILE
    row = lambda c: pl.BlockSpec((tm, c), lambda i: (i, 0))
    ins = [x2d, nw, w, wdt, qnw, knw, cw, cb, dtb]
    in_specs = [row(D_MODEL)] + [_const_spec(a.shape) for a in ins[1:]]
    out_shape = [jax.ShapeDtypeStruct((m, ATT_WIDTH), BF16)] * 3 + [
        jax.ShapeDtypeStruct((m, SSM_WIDTH), BF16),
        jax.ShapeDtypeStruct((m, XBC_WIDTH), BF16),
        jax.ShapeDtypeStruct((m, LANES), F32)]
    out_specs = [row(ATT_WIDTH)] * 3 + [row(SSM_WIDTH), row(XBC_WIDTH), row(LANES)]
    nchunk = XBC_WIDTH // LANES
    return pl.pallas_call(
        functools.partial(_inproj_kernel, tm=tm, tiles_per_seq=seq // tm),
        grid=(m // tm,), in_specs=in_specs, out_specs=out_specs, out_shape=out_shape,
        scratch_shapes=[pltpu.VMEM((nchunk, tm + SUBLANES, LANES), F32),
                        pltpu.VMEM((nchunk, tm, LANES), F32)],
        compiler_params=_params(("arbitrary",)), name="inproj")(*ins)


def _attn_kernel(q_ref, k_ref, v_ref, lq1_ref, lk1_ref, lq2_ref, lk2_ref, sw_ref, o_ref,
                 vext, mst, acc, *, t, lam_init):
    i = pl.program_id(1)
    nh = N_ATT_HEADS
    head = lambda hh: slice(hh * ATT_V_DIM, (hh + 1) * ATT_V_DIM)

    @pl.when(i == 0)
    def _():
        for hh in range(nh):
            vext[hh, :, 0:ATT_V_DIM] = v_ref[0, :, head(hh)]
            vext[hh, :, ATT_V_DIM:] = jnp.ones((vext.shape[1], ATT_V_DIM), BF16)

    lane = lax.broadcasted_iota(jnp.int32, (t, ATT_V_DIM), 1)
    qs = []
    for hh in range(nh):
        q = q_ref[0, :, head(hh)]
        zero = jnp.zeros_like(q)
        qs.append(jnp.where(lane < ATT_HEAD_DIM, q, zero))
        qs.append(jnp.where(lane >= ATT_HEAD_DIM, q, zero))

    mst[...] = jnp.full(mst.shape, -1e30, F32)
    acc[...] = jnp.zeros(acc.shape, F32)

    def update(s, vb, idx):
        m_old = mst[idx]
        m_new = jnp.maximum(m_old, jnp.max(s, axis=-1, keepdims=True))
        alpha = jnp.exp2(m_old - m_new)
        p = jnp.exp2(s - jnp.tile(m_new, (1, t // LANES)))
        acc[idx] = jnp.tile(alpha, (1, 2)) * acc[idx] + _dot(p.astype(BF16), vb)
        mst[idx] = m_new

    def block(j, mask):
        r0 = pl.multiple_of(j * t, t)
        for hh in range(nh):
            kb = k_ref[0, pl.ds(r0, t), head(hh)]
            vb = vext[hh, pl.ds(r0, t), :]
            for c in range(2):
                s = _dot_nt(qs[2 * hh + c], kb)
                if mask is not None:
                    s = jnp.where(mask, s, -jnp.inf)
                update(s, vb, 2 * hh + c)

    def body(j, carry):
        block(2 * j, None)
        block(2 * j + 1, None)
        return carry

    lax.fori_loop(0, i // 2, body, 0)

    @pl.when(i % 2 == 1)
    def _():
        block(i - 1, None)

    rr = lax.broadcasted_iota(jnp.int32, (t, t), 0)
    cc = lax.broadcasted_iota(jnp.int32, (t, t), 1)
    block(i, rr >= cc)

    lam = (jnp.exp(jnp.sum(lq1_ref[...] * lk1_ref[...], axis=-1, keepdims=True))
           - jnp.exp(jnp.sum(lq2_ref[...] * lk2_ref[...], axis=-1, keepdims=True)) + lam_init)
    for hh in range(nh):
        a1 = acc[2 * hh]
        a2 = acc[2 * hh + 1]
        o = (a1[:, 0:ATT_V_DIM] / a1[:, ATT_V_DIM:]
             - lam * (a2[:, 0:ATT_V_DIM] / a2[:, ATT_V_DIM:]))
        o_ref[0, :, head(hh)] = (_rms(o, sw_ref[...]) * (1.0 - lam_init)).astype(BF16)


def _attention(q, k, v, lq1, lk1, lq2, lk2, sw, *, lam_init):
    b, s, _ = q.shape
    t = ATT_TILE
    qspec = pl.BlockSpec((1, t, ATT_WIDTH), lambda bi, i: (bi, i, 0))
    kvspec = pl.BlockSpec((1, s, ATT_WIDTH), lambda bi, i: (bi, 0, 0))
    small = [lq1, lk1, lq2, lk2, sw]
    return pl.pallas_call(
        functools.partial(_attn_kernel, t=t, lam_init=lam_init),
        grid=(b, s // t),
        in_specs=[qspec, kvspec, kvspec] + [_const_spec(a.shape) for a in small],
        out_specs=qspec, out_shape=jax.ShapeDtypeStruct((b, s, ATT_WIDTH), BF16),
        scratch_shapes=[pltpu.VMEM((N_ATT_HEADS, s, 2 * ATT_V_DIM), BF16),
                        pltpu.VMEM((2 * N_ATT_HEADS, t, LANES), F32),
                        pltpu.VMEM((2 * N_ATT_HEADS, t, 2 * ATT_V_DIM), F32)],
        compiler_params=_params(("arbitrary", "arbitrary")),
        name="diff_attn")(q, k, v, *small)


def _expand_heads(v, sel):
    hi, lo, _ = _split3(v)
    return _dot(hi, sel) + _dot(lo, sel)


def _ssd_kernel(xs_ref, b_ref, c_ref, dt_ref, z_ref, alog_ref, dexp_ref, nw_ref, y_ref, state,
                *, tc, tiles_per_seq):
    i = pl.program_id(0)
    t = SSM_CHUNK
    gw = SSM_WIDTH // SSM_GROUPS
    hpg = SSM_HEADS // SSM_GROUPS

    @pl.when(i % tiles_per_seq == 0)
    def _():
        state[...] = jnp.zeros(state.shape, F32)

    a_head = -jnp.exp(alog_ref[...])
    rr = lax.broadcasted_iota(jnp.int32, (t, t), 0)
    cc = lax.broadcasted_iota(jnp.int32, (t, t), 1)
    tril = rr >= cc
    tri = tril.astype(BF16)
    lo_half = cc < SSM_HEAD_DIM
    sel = (lax.broadcasted_iota(jnp.int32, (LANES, SSM_WIDTH), 1) // SSM_HEAD_DIM
           == lax.broadcasted_iota(jnp.int32, (LANES, SSM_WIDTH), 0)).astype(BF16)

    def chunk(ci):
        rows = slice(ci * t, (ci + 1) * t)
        xs = xs_ref[rows, :].astype(F32)
        dt = dt_ref[rows, :]
        a = dt * a_head
        p0, p1, p2 = _split3(a)
        acum = _dot(tri, p0) + _dot(tri, p1) + _dot(tri, p2)
        acum_t = acum.T
        dtx = _expand_heads(dt, sel)
        acum_x = _expand_heads(acum, sel)
        alast_x = acum_x[t - 1:t, :]
        ea = jnp.exp(acum_x)
        dec = jnp.exp(alast_x - acum_x)
        elast = jnp.exp(alast_x)
        x = xs * dtx
        xb = x.astype(BF16)
        xdec = (x * dec).astype(BF16)
        ys = []
        for g in range(SSM_GROUPS):
            gs = slice(g * gw, (g + 1) * gw)
            bg = b_ref[rows, g * SSM_STATE:(g + 1) * SSM_STATE]
            cg = c_ref[rows, g * SSM_STATE:(g + 1) * SSM_STATE]
            cb = _dot_nt(cg, bg)
            st = state[g]
            y_g = _dot(cg, st.astype(BF16)) * ea[:, gs]
            pairs = []
            for pr in range(hpg // 2):
                xp = xb[:, g * gw + pr * LANES:g * gw + (pr + 1) * LANES]
                res = []
                for hh in range(2):
                    hd = g * hpg + 2 * pr + hh
                    diff = acum[:, hd:hd + 1] - acum_t[hd:hd + 1, :]
                    lm = jnp.exp(jnp.where(tril, diff, -jnp.inf))
                    res.append(_dot((cb * lm).astype(BF16), xp))
                pairs.append(jnp.where(lo_half, res[0], res[1]))
            ys.append(y_g + jnp.concatenate(pairs, axis=1))
            state[g] = st * elast[:, gs] + _dot_tn(bg, xdec[:, gs])
        y = jnp.concatenate(ys, axis=1) + xs * dexp_ref[...]
        y = y * z_ref[rows, :].astype(F32)
        outs = []
        for g in range(SSM_GROUPS):
            gs = slice(g * gw, (g + 1) * gw)
            outs.append(_rms(y[:, gs], nw_ref[:, gs]))
        y_ref[rows, :] = jnp.concatenate(outs, axis=1).astype(BF16)

    for ci in range(tc // t):
        chunk(ci)


def _ssd(xbc, dt, zs, alog, dexp, nw, *, seq):
    m = xbc.shape[0]
    tc = SSD_TILE
    bc_w = SSM_GROUPS * SSM_STATE
    ins = [xbc, xbc, xbc, dt, zs, alog, dexp, nw]
    in_specs = [pl.BlockSpec((tc, SSM_WIDTH), lambda i: (i, 0)),
                pl.BlockSpec((tc, bc_w), lambda i: (i, SSM_WIDTH // bc_w)),
                pl.BlockSpec((tc, bc_w), lambda i: (i, SSM_WIDTH // bc_w + 1)),
                pl.BlockSpec((tc, LANES), lambda i: (i, 0)),
                pl.BlockSpec((tc, SSM_WIDTH), lambda i: (i, 0))] + [
                    _const_spec(a.shape) for a in ins[5:]]
    return pl.pallas_call(
        functools.partial(_ssd_kernel, tc=tc, tiles_per_seq=seq // tc),
        grid=(m // tc,), in_specs=in_specs,
        out_specs=pl.BlockSpec((tc, SSM_WIDTH), lambda i: (i, 0)),
        out_shape=jax.ShapeDtypeStruct((m, SSM_WIDTH), BF16),
        scratch_shapes=[pltpu.VMEM((SSM_GROUPS, SSM_STATE, SSM_WIDTH // SSM_GROUPS), F32)],
        compiler_params=_params(("arbitrary",)), name="ssd")(*ins)


def _ffn_body(xs, nw_ref, up_ref, cw_ref, cb_ref, down_ref,
              o_ref, gbuf, vbuf, abuf, act, *, tm, tiles_per_seq):
    i = pl.program_id(0)
    hs = [_rms(x, nw_ref[...]).astype(BF16) for x in xs]
    nchunk = FFN_DIM // LANES
    per_dot = MXU_DIM // LANES

    @pl.when(i % tiles_per_seq == 0)
    def _():
        gbuf[:, 0:SUBLANES, :] = jnp.zeros((nchunk, SUBLANES, LANES), F32)
        vbuf[:, 0:SUBLANES, :] = jnp.zeros((nchunk, SUBLANES, LANES), F32)

    for d in range(FFN_DIM // MXU_DIM):
        c0 = d * MXU_DIM
        for g, h in enumerate(hs):
            r0 = SUBLANES + g * FFN_ROW_GROUP
            r = _dot(h, up_ref[:, 2 * c0:2 * c0 + 2 * MXU_DIM])
            for u in range(per_dot):
                c = d * per_dot + u
                gbuf[c, r0:r0 + FFN_ROW_GROUP, :] = r[:, _lane_chunk(u)]
                vbuf[c, r0:r0 + FFN_ROW_GROUP, :] = r[:, _lane_chunk(per_dot + u)]
        for u in range(per_dot):
            c = d * per_dot + u
            for row in _strided_rows(tm):
                g = _conv_rows(gbuf, c, cw_ref, cb_ref, c, row, SUBLANES, FFN_CONV)
                v = _conv_rows(vbuf, c, cw_ref, cb_ref, nchunk + c, row, SUBLANES, FFN_CONV)
                abuf[c, pl.ds(row, SUBLANES, stride=CONV_STRIDE), :] = _silu(g) * v
            act[:, _lane_chunk(c)] = abuf[c].astype(BF16)
            gbuf[c, 0:SUBLANES, :] = gbuf[c, tm:tm + SUBLANES, :]
            vbuf[c, 0:SUBLANES, :] = vbuf[c, tm:tm + SUBLANES, :]

    for g, x in enumerate(xs):
        rows = slice(g * FFN_ROW_GROUP, (g + 1) * FFN_ROW_GROUP)
        o_ref[rows, :] = x + _dot(act[rows, :], down_ref[...])


def _row_groups(tm):
    return [slice(g * FFN_ROW_GROUP, (g + 1) * FFN_ROW_GROUP) for g in range(tm // FFN_ROW_GROUP)]


def _ffn_kernel(x_ref, *rest, tm, **kw):
    _ffn_body([x_ref[rows, :] for rows in _row_groups(tm)], *rest, tm=tm, **kw)


def _mix_ffn_kernel(x_ref, att_ref, y_ref, wo_ref, *rest, tm, **kw):
    xs = [x_ref[rows, :] + _dot(att_ref[rows, :], wo_ref[0:ATT_WIDTH, :])
          + _dot(y_ref[rows, :], wo_ref[ATT_WIDTH:, :]) for rows in _row_groups(tm)]
    _ffn_body(xs, *rest, tm=tm, **kw)


def _ffn(x2d, nw, up_all, cw, cb, down_all, *, seq, layer, mix=None):
    m = x2d.shape[0]
    tm = FFN_ROW_TILE
    nchunk = FFN_DIM // LANES
    row = lambda c: pl.BlockSpec((tm, c), lambda i: (i, 0))
    consts = [nw, up_all, cw, cb, down_all]
    const_specs = [_const_spec(nw.shape), _layer_spec(up_all.shape, layer), _const_spec(cw.shape),
                   _const_spec(cb.shape), _layer_spec(down_all.shape, layer)]
    if mix is None:
        body, ins, specs = _ffn_kernel, [x2d], [row(D_MODEL)]
    else:
        att, y, wo = mix
        body, ins = _mix_ffn_kernel, [x2d, att, y, wo]
        specs = [row(D_MODEL), row(ATT_WIDTH), row(SSM_WIDTH), _const_spec(wo.shape)]
    return pl.pallas_call(
        functools.partial(body, tm=tm, tiles_per_seq=seq // tm),
        grid=(m // tm,), in_specs=specs + const_specs,
        out_specs=row(D_MODEL), out_shape=jax.ShapeDtypeStruct((m, D_MODEL), F32),
        scratch_shapes=[pltpu.VMEM((nchunk, tm + SUBLANES, LANES), F32),
                        pltpu.VMEM((nchunk, tm + SUBLANES, LANES), F32),
                        pltpu.VMEM((nchunk, tm, LANES), F32),
                        pltpu.VMEM((tm, FFN_DIM), BF16)],
        compiler_params=_params(("arbitrary",)), name="conv_ffn")(*ins, *consts)


def _conformer_kernel(x_ref, nw_ref, w1_ref, b1_ref, dww_ref, dwb_ref,
                      lnw_ref, lnb_ref, w2_ref, b2_ref, o_ref, cbuf, ybuf, *, tm, tiles_per_seq):
    i = pl.program_id(0)
    nchunk = CONF_WIDTH // LANES
    groups = [(g * CONF_ROW_GROUP, (g + 1) * CONF_ROW_GROUP) for g in range(tm // CONF_ROW_GROUP)]

    @pl.when(i % tiles_per_seq == 0)
    def _():
        cbuf[:, 0:CONF_HEAD, :] = jnp.zeros((nchunk, CONF_HEAD, LANES), F32)

    for r0, r1 in groups:
        h = _rms(x_ref[r0:r1, :], nw_ref[...]).astype(BF16)
        ua = _dot(h, w1_ref[:, 0:CONF_WIDTH]) + b1_ref[:, 0:CONF_WIDTH]
        ub = _dot(h, w1_ref[:, CONF_WIDTH:]) + b1_ref[:, CONF_WIDTH:]
        glu = ua * _sigmoid(ub)
        for c in range(nchunk):
            cbuf[c, CONF_HEAD + r0:CONF_HEAD + r1, :] = glu[:, _lane_chunk(c)]

    def conv_chunk(c, carry):
        for row in _strided_rows(tm):
            ybuf[c, pl.ds(row, SUBLANES, stride=CONV_STRIDE), :] = _conv_rows(
                cbuf, c, dww_ref, dwb_ref, c, row, CONF_HEAD, CONF_K)
        cbuf[c, 0:CONF_HEAD, :] = cbuf[c, tm:tm + CONF_HEAD, :]
        return carry

    lax.fori_loop(0, nchunk, conv_chunk, 0)

    for r0, r1 in groups:
        u = jnp.concatenate([ybuf[c, r0:r1, :] for c in range(nchunk)], axis=1)
        mu = jnp.mean(u, axis=-1, keepdims=True)
        d = u - mu
        var = jnp.mean(d * d, axis=-1, keepdims=True)
        y = _silu(d * lax.rsqrt(var + LN_EPS) * lnw_ref[...] + lnb_ref[...]).astype(BF16)
        o_ref[r0:r1, :] = x_ref[r0:r1, :] + _dot(y, w2_ref[...]) + b2_ref[...]


def _conformer(x2d, nw, w1, b1, dww, dwb, lnw, lnb, w2, b2, *, seq):
    m = x2d.shape[0]
    tm = CONF_ROW_TILE
    nchunk = CONF_WIDTH // LANES
    row = pl.BlockSpec((tm, D_MODEL), lambda i: (i, 0))
    ins = [x2d, nw, w1, b1, dww, dwb, lnw, lnb, w2, b2]
    return pl.pallas_call(
        functools.partial(_conformer_kernel, tm=tm, tiles_per_seq=seq // tm),
        grid=(m // tm,), in_specs=[row] + [_const_spec(a.shape) for a in ins[1:]],
        out_specs=row, out_shape=jax.ShapeDtypeStruct((m, D_MODEL), F32),
        scratch_shapes=[pltpu.VMEM((nchunk, tm + CONF_HEAD, LANES), F32),
                        pltpu.VMEM((nchunk, tm, LANES), F32)],
        compiler_params=_params(("arbitrary",)), name="conformer")(*ins)


def _row(v):
    return v.reshape(1, -1).astype(F32)


def _chunk_taps(w):
    k, c = w.shape
    return w.astype(F32).reshape(k, c // LANES, LANES).transpose(1, 0, 2)


def _chunk_bias(b):
    return b.astype(F32).reshape(-1, 1, LANES)


def _pad_lanes(v, width=LANES):
    return jnp.pad(v, ((0, 0), (0, width - v.shape[1])))


def _mixer_branches(x2d, seq, lam_init, mix_norm_w, w_in, q_norm_w, k_norm_w, lq1, lk1, lq2, lk2,
                    attn_subln_w, conv_w, conv_b, dt_bias, a_log, d_skip, ssm_norm_w):
    m = x2d.shape[0]
    bsz = m // seq
    dt_col = 3 * ATT_WIDTH + SSM_WIDTH + XBC_WIDTH
    wdt = _pad_lanes(w_in[:, dt_col:].astype(BF16))
    reps = ATT_WIDTH // ATT_HEAD_DIM
    q, k, v, zs, xbc, dt = _inproj(
        x2d, _row(mix_norm_w), w_in.astype(BF16), wdt,
        _row(jnp.tile(q_norm_w, reps)), _row(jnp.tile(k_norm_w, reps)),
        _chunk_taps(conv_w), _chunk_bias(conv_b), _pad_lanes(_row(dt_bias)), seq=seq)
    shp = (bsz, seq, ATT_WIDTH)
    att = _attention(q.reshape(shp), k.reshape(shp), v.reshape(shp),
                     _row(lq1), _row(lk1), _row(lq2), _row(lk2), _row(attn_subln_w),
                     lam_init=lam_init).reshape(m, ATT_WIDTH)
    y = _ssd(xbc, dt, zs, _pad_lanes(_row(a_log)), _row(jnp.repeat(d_skip, SSM_HEAD_DIM)),
             _row(ssm_norm_w), seq=seq)
    return att, y


def _ffn_layer(x2d, seq, layer, norm_w, up_all, conv_w, conv_b, down_all, mix=None):
    return _ffn(x2d, _row(norm_w), up_all, _chunk_taps(conv_w), _chunk_bias(conv_b),
                down_all, seq=seq, layer=layer, mix=mix)


def _conformer_layer(x2d, seq, norm_w, pw1_w, pw1_b, dw_w, dw_b, ln_w, ln_b, pw2_w, pw2_b):
    return _conformer(x2d, _row(norm_w), pw1_w.astype(BF16), _row(pw1_b),
                      _chunk_taps(dw_w), _chunk_bias(dw_b),
                      _row(ln_w), _row(ln_b), pw2_w.astype(BF16), _row(pw2_b), seq=seq)


def kernel(x, mix_norm_w, w_in, q_norm_w, k_norm_w, lambda_q1, lambda_k1, lambda_q2, lambda_k2,
           attn_subln_w, ssm_conv_w, ssm_conv_b, ssm_dt_bias, ssm_A_log, ssm_D, ssm_norm_w, w_out,
           conf_norm_w, conf_pw1_w, conf_pw1_b, conf_dw_w, conf_dw_b, conf_ln_w, conf_ln_b,
           conf_pw2_w, conf_pw2_b, ffn_norm_w, ffn_up_w, ffn_conv_w, ffn_conv_b, ffn_down_w):
    bsz, seq, d = x.shape
    depth = ffn_norm_w.shape[0]
    up_all = ffn_up_w.astype(BF16).reshape(depth, d, 2, FFN_DIM // MXU_DIM, MXU_DIM)
    up_all = up_all.transpose(0, 1, 3, 2, 4).reshape(depth, d, 2 * FFN_DIM)
    down_all = ffn_down_w.astype(BF16)
    h = x.reshape(bsz * seq, d)
    for i in range(depth):
        mix = None
        if i % 2 == 0:
            e = i // 2
            lam_init = 0.8 - 0.6 * math.exp(-0.3 * i)
            att, y = _mixer_branches(
                h, seq, lam_init, mix_norm_w[e], w_in[e], q_norm_w[e], k_norm_w[e],
                lambda_q1[e], lambda_k1[e], lambda_q2[e], lambda_k2[e], attn_subln_w[e],
                ssm_conv_w[e], ssm_conv_b[e], ssm_dt_bias[e], ssm_A_log[e], ssm_D[e],
                ssm_norm_w[e])
            mix = (att, y, w_out[e].astype(BF16))
        else:
            o = i // 2
            h = _conformer_layer(h, seq, conf_norm_w[o], conf_pw1_w[o], conf_pw1_b[o],
                                 conf_dw_w[o], conf_dw_b[o], conf_ln_w[o], conf_ln_b[o],
                                 conf_pw2_w[o], conf_pw2_b[o])
        h = _ffn_layer(h, seq, i, ffn_norm_w[i], up_all, ffn_conv_w[i], ffn_conv_b[i],
                       down_all, mix=mix)
    return h.reshape(bsz, seq, d)
```

```python
import functools
import math

import jax
import jax.numpy as jnp
from jax import lax
from jax.experimental import pallas as pl
from jax.experimental.pallas import tpu as pltpu

F32 = jnp.float32
BF16 = jnp.bfloat16

D_MODEL = 1024
N_ATT_HEADS = 4
ATT_HEAD_DIM = 64
ATT_V_DIM = 2 * ATT_HEAD_DIM
ATT_WIDTH = N_ATT_HEADS * ATT_V_DIM
ATT_SCALE = ATT_HEAD_DIM ** -0.5
LOG2_E = math.log2(math.e)
SSM_WIDTH = D_MODEL - ATT_WIDTH
SSM_HEAD_DIM = 64
SSM_HEADS = SSM_WIDTH // SSM_HEAD_DIM
SSM_GROUPS = 2
SSM_STATE = 128
SSM_CONV = 4
SSM_CHUNK = 128
XBC_WIDTH = SSM_WIDTH + 2 * SSM_GROUPS * SSM_STATE
CONF_WIDTH = D_MODEL
CONF_K = 31
FFN_DIM = 2816
FFN_CONV = 3
RMS_EPS = 1e-6
LN_EPS = 1e-5

LANES = 128
SUBLANES = 8
MXU_DIM = 256
VMEM_LIMIT = 56 * 1024 * 1024

ROW_TILE = 512
ROW_GROUP = 256
FFN_ROW_TILE = 512
FFN_ROW_GROUP = 512
CONF_ROW_TILE = 512
CONF_ROW_GROUP = 256
ATT_TILE = 512
SSD_TILE = 1024
CONV_STRIDE = 4
CONV_BLOCK = SUBLANES * CONV_STRIDE
CONF_HEAD = 32


def _dot(a, b):
    return jnp.dot(a, b, preferred_element_type=F32)


def _dot_nt(a, b):
    return lax.dot_general(a, b, (((1,), (1,)), ((), ())), preferred_element_type=F32)


def _dot_tn(a, b):
    return lax.dot_general(a, b, (((0,), (0,)), ((), ())), preferred_element_type=F32)


def _rms(x, w):
    ms = jnp.mean(x * x, axis=-1, keepdims=True)
    return x * lax.rsqrt(ms + RMS_EPS) * w


def _sigmoid(x):
    return 1.0 / (1.0 + jnp.exp(-x))


def _silu(x):
    return x * _sigmoid(x)


def _softplus(x):
    return jnp.maximum(x, 0.0) + jnp.log(1.0 + jnp.exp(-jnp.abs(x)))


def _split3(x):
    a = x.astype(BF16)
    r = x - a.astype(F32)
    b = r.astype(BF16)
    c = (r - b.astype(F32)).astype(BF16)
    return a, b, c


def _const_spec(shape):
    nd = len(shape)
    return pl.BlockSpec(shape, lambda *_: (0,) * nd, pipeline_mode=pl.Buffered(1))


def _layer_spec(shape, layer):
    nd = len(shape)
    return pl.BlockSpec((None,) + tuple(shape[1:]), lambda *_: (layer,) + (0,) * (nd - 1),
                        pipeline_mode=pl.Buffered(1))


def _params(sem):
    return pltpu.CompilerParams(dimension_semantics=sem, vmem_limit_bytes=VMEM_LIMIT)


def _lane_chunk(c):
    return slice(c * LANES, (c + 1) * LANES)


def _conv_rows(buf, c, w_ref, b_ref, wc, row, head, ntaps):
    first = head - (ntaps - 1) + row
    acc = b_ref[wc] + w_ref[wc, 0:1, :] * buf[c, pl.ds(first, SUBLANES, stride=CONV_STRIDE), :]
    for k in range(1, ntaps):
        acc = acc + w_ref[wc, k:k + 1, :] * buf[c, pl.ds(first + k, SUBLANES, stride=CONV_STRIDE), :]
    return acc


def _strided_rows(tm):
    return [blk * CONV_BLOCK + b for blk in range(tm // CONV_BLOCK) for b in range(CONV_STRIDE)]


def _inproj_kernel(x_ref, nw_ref, w_ref, wdt_ref, qnw_ref, knw_ref, cw_ref, cb_ref,
                   dtb_ref, q_ref, k_ref, v_ref, z_ref, xbc_ref, dt_ref, cbuf, ybuf,
                   *, tm, tiles_per_seq):
    i = pl.program_id(0)
    nchunk = XBC_WIDTH // LANES
    col_q, col_k, col_v, col_z, col_x = (
        0, ATT_WIDTH, 2 * ATT_WIDTH, 3 * ATT_WIDTH, 3 * ATT_WIDTH + SSM_WIDTH)
    groups = [(g * ROW_GROUP, (g + 1) * ROW_GROUP) for g in range(tm // ROW_GROUP)]

    @pl.when(i % tiles_per_seq == 0)
    def _():
        cbuf[:, 0:SUBLANES, :] = jnp.zeros((nchunk, SUBLANES, LANES), F32)

    hs = []
    for r0, r1 in groups:
        h = _rms(x_ref[r0:r1, :], nw_ref[...]).astype(BF16)
        raw = _dot(h, w_ref[:, col_x:col_x + XBC_WIDTH])
        for c in range(nchunk):
            cbuf[c, SUBLANES + r0:SUBLANES + r1, :] = raw[:, _lane_chunk(c)]
        hs.append(h)
    for c in range(nchunk):
        for row in _strided_rows(tm):
            acc = _conv_rows(cbuf, c, cw_ref, cb_ref, c, row, SUBLANES, SSM_CONV)
            ybuf[c, pl.ds(row, SUBLANES, stride=CONV_STRIDE), :] = _silu(acc)
        xbc_ref[:, _lane_chunk(c)] = ybuf[c].astype(BF16)
        cbuf[c, 0:SUBLANES, :] = cbuf[c, tm:tm + SUBLANES, :]

    lo_half = lax.broadcasted_iota(jnp.int32, (ROW_GROUP, ATT_V_DIM), 1) < ATT_HEAD_DIM

    def qk_norm(h, col, nw):
        p = _dot(h, w_ref[:, col:col + ATT_WIDTH])
        outs = []
        for hh in range(N_ATT_HEADS):
            ph = p[:, hh * ATT_V_DIM:(hh + 1) * ATT_V_DIM]
            sq = ph * ph
            s_lo = jnp.sum(jnp.where(lo_half, sq, 0.0), axis=-1, keepdims=True)
            s_hi = jnp.sum(jnp.where(lo_half, 0.0, sq), axis=-1, keepdims=True)
            ms = jnp.where(lo_half, s_lo, s_hi) * (1.0 / ATT_HEAD_DIM)
            outs.append(ph * lax.rsqrt(ms + RMS_EPS))
        return jnp.concatenate(outs, axis=1) * nw

    for (r0, r1), h in zip(groups, hs):
        q_ref[r0:r1, :] = (qk_norm(h, col_q, qnw_ref[...]) * (ATT_SCALE * LOG2_E)).astype(BF16)
        k_ref[r0:r1, :] = qk_norm(h, col_k, knw_ref[...]).astype(BF16)
        v_ref[r0:r1, :] = _dot(h, w_ref[:, col_v:col_v + ATT_WIDTH]).astype(BF16)
        z_ref[r0:r1, :] = _silu(_dot(h, w_ref[:, col_z:col_z + SSM_WIDTH])).astype(BF16)
        dt_ref[r0:r1, :] = _softplus(_dot(h, wdt_ref[...]) + dtb_ref[...])


def _inproj(x2d, nw, w, wdt, qnw, knw, cw, cb, dtb, *, seq):
    m = x2d.shape[0]
    tm = ROW_TILE
    row = lambda c: pl.BlockSpec((tm, c), lambda i: (i, 0))
    ins = [x2d, nw, w, wdt, qnw, knw, cw, cb, dtb]
    in_specs = [row(D_MODEL)] + [_const_spec(a.shape) for a in ins[1:]]
    out_shape = [jax.ShapeDtypeStruct((m, ATT_WIDTH), BF16)] * 3 + [
        jax.ShapeDtypeStruct((m, SSM_WIDTH), BF16),
        jax.ShapeDtypeStruct((m, XBC_WIDTH), BF16),
        jax.ShapeDtypeStruct((m, LANES), F32)]
    out_specs = [row(ATT_WIDTH)] * 3 + [row(SSM_WIDTH), row(XBC_WIDTH), row(LANES)]
    nchunk = XBC_WIDTH // LANES
    return pl.pallas_call(
        functools.partial(_inproj_kernel, tm=tm, tiles_per_seq=seq // tm),
        grid=(m // tm,), in_specs=in_specs, out_specs=out_specs, out_shape=out_shape,
        scratch_shapes=[pltpu.VMEM((nchunk, tm + SUBLANES, LANES), F32),
                        pltpu.VMEM((nchunk, tm, LANES), F32)],
        compiler_params=_params(("arbitrary",)), name="inproj")(*ins)


def _attn_kernel(q_ref, k_ref, v_ref, lq1_ref, lk1_ref, lq2_ref, lk2_ref, sw_ref, o_ref,
                 vext, mst, acc, *, t, lam_init):
    i = pl.program_id(1)
    nh = N_ATT_HEADS
    head = lambda hh: slice(hh * ATT_V_DIM, (hh + 1) * ATT_V_DIM)

    @pl.when(i == 0)
    def _():
        for hh in range(nh):
            vext[hh, :, 0:ATT_V_DIM] = v_ref[0, :, head(hh)]
            vext[hh, :, ATT_V_DIM:] = jnp.ones((vext.shape[1], ATT_V_DIM), BF16)

    lane = lax.broadcasted_iota(jnp.int32, (t, ATT_V_DIM), 1)
    qs = []
    for hh in range(nh):
        q = q_ref[0, :, head(hh)]
        zero = jnp.zeros_like(q)
        qs.append(jnp.where(lane < ATT_HEAD_DIM, q, zero))
        qs.append(jnp.where(lane >= ATT_HEAD_DIM, q, zero))

    mst[...] = jnp.full(mst.shape, -1e30, F32)
    acc[...] = jnp.zeros(acc.shape, F32)

    def update(s, vb, idx):
        m_old = mst[idx]
        m_new = jnp.maximum(m_old, jnp.max(s, axis=-1, keepdims=True))
        alpha = jnp.exp2(m_old - m_new)
        p = jnp.exp2(s - jnp.tile(m_new, (1, t // LANES)))
        acc[idx] = jnp.tile(alpha, (1, 2)) * acc[idx] + _dot(p.astype(BF16), vb)
        mst[idx] = m_new

    def block(j, mask):
        r0 = pl.multiple_of(j * t, t)
        for hh in range(nh):
            kb = k_ref[0, pl.ds(r0, t), head(hh)]
            vb = vext[hh, pl.ds(r0, t), :]
            for c in range(2):
                s = _dot_nt(qs[2 * hh + c], kb)
                if mask is not None:
                    s = jnp.where(mask, s, -jnp.inf)
                update(s, vb, 2 * hh + c)

    def body(j, carry):
        block(2 * j, None)
        block(2 * j + 1, None)
        return carry

    lax.fori_loop(0, i // 2, body, 0)

    @pl.when(i % 2 == 1)
    def _():
        block(i - 1, None)

    rr = lax.broadcasted_iota(jnp.int32, (t, t), 0)
    cc = lax.broadcasted_iota(jnp.int32, (t, t), 1)
    block(i, rr >= cc)

    lam = (jnp.exp(jnp.sum(lq1_ref[...] * lk1_ref[...], axis=-1, keepdims=True))
           - jnp.exp(jnp.sum(lq2_ref[...] * lk2_ref[...], axis=-1, keepdims=True)) + lam_init)
    for hh in range(nh):
        a1 = acc[2 * hh]
        a2 = acc[2 * hh + 1]
        o = (a1[:, 0:ATT_V_DIM] / a1[:, ATT_V_DIM:]
             - lam * (a2[:, 0:ATT_V_DIM] / a2[:, ATT_V_DIM:]))
        o_ref[0, :, head(hh)] = (_rms(o, sw_ref[...]) * (1.0 - lam_init)).astype(BF16)


def _attention(q, k, v, lq1, lk1, lq2, lk2, sw, *, lam_init):
    b, s, _ = q.shape
    t = ATT_TILE
    qspec = pl.BlockSpec((1, t, ATT_WIDTH), lambda bi, i: (bi, i, 0))
    kvspec = pl.BlockSpec((1, s, ATT_WIDTH), lambda bi, i: (bi, 0, 0))
    small = [lq1, lk1, lq2, lk2, sw]
    return pl.pallas_call(
        functools.partial(_attn_kernel, t=t, lam_init=lam_init),
        grid=(b, s // t),
        in_specs=[qspec, kvspec, kvspec] + [_const_spec(a.shape) for a in small],
        out_specs=qspec, out_shape=jax.ShapeDtypeStruct((b, s, ATT_WIDTH), BF16),
        scratch_shapes=[pltpu.VMEM((N_ATT_HEADS, s, 2 * ATT_V_DIM), BF16),
                        pltpu.VMEM((2 * N_ATT_HEADS, t, LANES), F32),
                        pltpu.VMEM((2 * N_ATT_HEADS, t, 2 * ATT_V_DIM), F32)],
        compiler_params=_params(("arbitrary", "arbitrary")),
        name="diff_attn")(q, k, v, *small)


def _expand_heads(v, sel):
    hi, lo, _ = _split3(v)
    return _dot(hi, sel) + _dot(lo, sel)


def _ssd_kernel(xs_ref, b_ref, c_ref, dt_ref, z_ref, alog_ref, dexp_ref, nw_ref, y_ref, state,
                *, tc, tiles_per_seq):
    i = pl.program_id(0)
    t = SSM_CHUNK
    gw = SSM_WIDTH // SSM_GROUPS
    hpg = SSM_HEADS // SSM_GROUPS

    @pl.when(i % tiles_per_seq == 0)
    def _():
        state[...] = jnp.zeros(state.shape, F32)

    a_head = -jnp.exp(alog_ref[...])
    rr = lax.broadcasted_iota(jnp.int32, (t, t), 0)
    cc = lax.broadcasted_iota(jnp.int32, (t, t), 1)
    tril = rr >= cc
    tri = tril.astype(BF16)
    lo_half = cc < SSM_HEAD_DIM
    sel = (lax.broadcasted_iota(jnp.int32, (LANES, SSM_WIDTH), 1) // SSM_HEAD_DIM
           == lax.broadcasted_iota(jnp.int32, (LANES, SSM_WIDTH), 0)).astype(BF16)

    def chunk(ci):
        rows = slice(ci * t, (ci + 1) * t)
        xs = xs_ref[rows, :].astype(F32)
        dt = dt_ref[rows, :]
        a = dt * a_head
        p0, p1, p2 = _split3(a)
        acum = _dot(tri, p0) + _dot(tri, p1) + _dot(tri, p2)
        acum_t = acum.T
        dtx = _expand_heads(dt, sel)
        acum_x = _expand_heads(acum, sel)
        alast_x = acum_x[t - 1:t, :]
        ea = jnp.exp(acum_x)
        dec = jnp.exp(alast_x - acum_x)
        elast = jnp.exp(alast_x)
        x = xs * dtx
        xb = x.astype(BF16)
        xdec = (x * dec).astype(BF16)
        ys = []
        for g in range(SSM_GROUPS):
            gs = slice(g * gw, (g + 1) * gw)
            bg = b_ref[rows, g * SSM_STATE:(g + 1) * SSM_STATE]
            cg = c_ref[rows, g * SSM_STATE:(g + 1) * SSM_STATE]
            cb = _dot_nt(cg, bg)
            st = state[g]
            y_g = _dot(cg, st.astype(BF16)) * ea[:, gs]
            pairs = []
            for pr in range(hpg // 2):
                xp = xb[:, g * gw + pr * LANES:g * gw + (pr + 1) * LANES]
                res = []
                for hh in range(2):
                    hd = g * hpg + 2 * pr + hh
                    diff = acum[:, hd:hd + 1] - acum_t[hd:hd + 1, :]
                    lm = jnp.exp(jnp.where(tril, diff, -jnp.inf))
                    res.append(_dot((cb * lm).astype(BF16), xp))
                pairs.append(jnp.where(lo_half, res[0], res[1]))
            ys.append(y_g + jnp.concatenate(pairs, axis=1))
            state[g] = st * elast[:, gs] + _dot_tn(bg, xdec[:, gs])
        y = jnp.concatenate(ys, axis=1) + xs * dexp_ref[...]
        y = y * z_ref[rows, :].astype(F32)
        outs = []
        for g in range(SSM_GROUPS):
            gs = slice(g * gw, (g + 1) * gw)
            outs.append(_rms(y[:, gs], nw_ref[:, gs]))
        y_ref[rows, :] = jnp.concatenate(outs, axis=1).astype(BF16)

    for ci in range(tc // t):
        chunk(ci)


def _ssd(xbc, dt, zs, alog, dexp, nw, *, seq):
    m = xbc.shape[0]
    tc = SSD_TILE
    bc_w = SSM_GROUPS * SSM_STATE
    ins = [xbc, xbc, xbc, dt, zs, alog, dexp, nw]
    in_specs = [pl.BlockSpec((tc, SSM_WIDTH), lambda i: (i, 0)),
                pl.BlockSpec((tc, bc_w), lambda i: (i, SSM_WIDTH // bc_w)),
                pl.BlockSpec((tc, bc_w), lambda i: (i, SSM_WIDTH // bc_w + 1)),
                pl.BlockSpec((tc, LANES), lambda i: (i, 0)),
                pl.BlockSpec((tc, SSM_WIDTH), lambda i: (i, 0))] + [
                    _const_spec(a.shape) for a in ins[5:]]
    return pl.pallas_call(
        functools.partial(_ssd_kernel, tc=tc, tiles_per_seq=seq // tc),
        grid=(m // tc,), in_specs=in_specs,
        out_specs=pl.BlockSpec((tc, SSM_WIDTH), lambda i: (i, 0)),
        out_shape=jax.ShapeDtypeStruct((m, SSM_WIDTH), BF16),
        scratch_shapes=[pltpu.VMEM((SSM_GROUPS, SSM_STATE, SSM_WIDTH // SSM_GROUPS), F32)],
        compiler_params=_params(("arbitrary",)), name="ssd")(*ins)


def _ffn_body(xs, nw_ref, up_ref, cw_ref, cb_ref, down_ref,
              o_ref, gbuf, vbuf, abuf, act, *, tm, tiles_per_seq):
    i = pl.program_id(0)
    hs = [_rms(x, nw_ref[...]).astype(BF16) for x in xs]
    nchunk = FFN_DIM // LANES
    per_dot = MXU_DIM // LANES

    @pl.when(i % tiles_per_seq == 0)
    def _():
        gbuf[:, 0:SUBLANES, :] = jnp.zeros((nchunk, SUBLANES, LANES), F32)
        vbuf[:, 0:SUBLANES, :] = jnp.zeros((nchunk, SUBLANES, LANES), F32)

    for d in range(FFN_DIM // MXU_DIM):
        c0 = d * MXU_DIM
        for g, h in enumerate(hs):
            r0 = SUBLANES + g * FFN_ROW_GROUP
            rg = _dot(h, up_ref[:, c0:c0 + MXU_DIM])
            rv = _dot(h, up_ref[:, FFN_DIM + c0:FFN_DIM + c0 + MXU_DIM])
            for u in range(per_dot):
                c = d * per_dot + u
                gbuf[c, r0:r0 + FFN_ROW_GROUP, :] = rg[:, _lane_chunk(u)]
                vbuf[c, r0:r0 + FFN_ROW_GROUP, :] = rv[:, _lane_chunk(u)]
        for u in range(per_dot):
            c = d * per_dot + u
            for row in _strided_rows(tm):
                g = _conv_rows(gbuf, c, cw_ref, cb_ref, c, row, SUBLANES, FFN_CONV)
                v = _conv_rows(vbuf, c, cw_ref, cb_ref, nchunk + c, row, SUBLANES, FFN_CONV)
                abuf[c, pl.ds(row, SUBLANES, stride=CONV_STRIDE), :] = _silu(g) * v
            act[:, _lane_chunk(c)] = abuf[c].astype(BF16)
            gbuf[c, 0:SUBLANES, :] = gbuf[c, tm:tm + SUBLANES, :]
            vbuf[c, 0:SUBLANES, :] = vbuf[c, tm:tm + SUBLANES, :]

    for g, x in enumerate(xs):
        rows = slice(g * FFN_ROW_GROUP, (g + 1) * FFN_ROW_GROUP)
        o_ref[rows, :] = x + _dot(act[rows, :], down_ref[...])


def _row_groups(tm):
    return [slice(g * FFN_ROW_GROUP, (g + 1) * FFN_ROW_GROUP) for g in range(tm // FFN_ROW_GROUP)]


def _ffn_kernel(x_ref, *rest, tm, **kw):
    _ffn_body([x_ref[rows, :] for rows in _row_groups(tm)], *rest, tm=tm, **kw)


def _mix_ffn_kernel(x_ref, att_ref, y_ref, wo_ref, *rest, tm, **kw):
    xs = [x_ref[rows, :] + _dot(att_ref[rows, :], wo_ref[0:ATT_WIDTH, :])
          + _dot(y_ref[rows, :], wo_ref[ATT_WIDTH:, :]) for rows in _row_groups(tm)]
    _ffn_body(xs, *rest, tm=tm, **kw)


def _ffn(x2d, nw, up_all, cw, cb, down_all, *, seq, layer, mix=None):
    m = x2d.shape[0]
    tm = FFN_ROW_TILE
    nchunk = FFN_DIM // LANES
    row = lambda c: pl.BlockSpec((tm, c), lambda i: (i, 0))
    consts = [nw, up_all, cw, cb, down_all]
    const_specs = [_const_spec(nw.shape), _layer_spec(up_all.shape, layer), _const_spec(cw.shape),
                   _const_spec(cb.shape), _layer_spec(down_all.shape, layer)]
    if mix is None:
        body, ins, specs = _ffn_kernel, [x2d], [row(D_MODEL)]
    else:
        att, y, wo = mix
        body, ins = _mix_ffn_kernel, [x2d, att, y, wo]
        specs = [row(D_MODEL), row(ATT_WIDTH), row(SSM_WIDTH), _const_spec(wo.shape)]
    return pl.pallas_call(
        functools.partial(body, tm=tm, tiles_per_seq=seq // tm),
        grid=(m // tm,), in_specs=specs + const_specs,
        out_specs=row(D_MODEL), out_shape=jax.ShapeDtypeStruct((m, D_MODEL), F32),
        scratch_shapes=[pltpu.VMEM((nchunk, tm + SUBLANES, LANES), F32),
                        pltpu.VMEM((nchunk, tm + SUBLANES, LANES), F32),
                        pltpu.VMEM((nchunk, tm, LANES), F32),
                        pltpu.VMEM((tm, FFN_DIM), BF16)],
        compiler_params=_params(("arbitrary",)), name="conv_ffn")(*ins, *consts)


def _conformer_kernel(x_ref, nw_ref, w1_ref, b1_ref, dww_ref, dwb_ref,
                      lnw_ref, lnb_ref, w2_ref, b2_ref, o_ref, cbuf, ybuf, *, tm, tiles_per_seq):
    i = pl.program_id(0)
    nchunk = CONF_WIDTH // LANES
    groups = [(g * CONF_ROW_GROUP, (g + 1) * CONF_ROW_GROUP) for g in range(tm // CONF_ROW_GROUP)]

    @pl.when(i % tiles_per_seq == 0)
    def _():
        cbuf[:, 0:CONF_HEAD, :] = jnp.zeros((nchunk, CONF_HEAD, LANES), F32)

    for r0, r1 in groups:
        h = _rms(x_ref[r0:r1, :], nw_ref[...]).astype(BF16)
        ua = _dot(h, w1_ref[:, 0:CONF_WIDTH]) + b1_ref[:, 0:CONF_WIDTH]
        ub = _dot(h, w1_ref[:, CONF_WIDTH:]) + b1_ref[:, CONF_WIDTH:]
        glu = ua * _sigmoid(ub)
        for c in range(nchunk):
            cbuf[c, CONF_HEAD + r0:CONF_HEAD + r1, :] = glu[:, _lane_chunk(c)]

    def conv_chunk(c, carry):
        for row in _strided_rows(tm):
            ybuf[c, pl.ds(row, SUBLANES, stride=CONV_STRIDE), :] = _conv_rows(
                cbuf, c, dww_ref, dwb_ref, c, row, CONF_HEAD, CONF_K)
        cbuf[c, 0:CONF_HEAD, :] = cbuf[c, tm:tm + CONF_HEAD, :]
        return carry

    lax.fori_loop(0, nchunk, conv_chunk, 0)

    for r0, r1 in groups:
        u = jnp.concatenate([ybuf[c, r0:r1, :] for c in range(nchunk)], axis=1)
        mu = jnp.mean(u, axis=-1, keepdims=True)
        d = u - mu
        var = jnp.mean(d * d, axis=-1, keepdims=True)
        y = _silu(d * lax.rsqrt(var + LN_EPS) * lnw_ref[...] + lnb_ref[...]).astype(BF16)
        o_ref[r0:r1, :] = x_ref[r0:r1, :] + _dot(y, w2_ref[...]) + b2_ref[...]


def _conformer(x2d, nw, w1, b1, dww, dwb, lnw, lnb, w2, b2, *, seq):
    m = x2d.shape[0]
    tm = CONF_ROW_TILE
    nchunk = CONF_WIDTH // LANES
    row = pl.BlockSpec((tm, D_MODEL), lambda i: (i, 0))
    ins = [x2d, nw, w1, b1, dww, dwb, lnw, lnb, w2, b2]
    return pl.pallas_call(
        functools.partial(_conformer_kernel, tm=tm, tiles_per_seq=seq // tm),
        grid=(m // tm,), in_specs=[row] + [_const_spec(a.shape) for a in ins[1:]],
        out_specs=row, out_shape=jax.ShapeDtypeStruct((m, D_MODEL), F32),
        scratch_shapes=[pltpu.VMEM((nchunk, tm + CONF_HEAD, LANES), F32),
                        pltpu.VMEM((nchunk, tm, LANES), F32)],
        compiler_params=_params(("arbitrary",)), name="conformer")(*ins)


def _row(v):
    return v.reshape(1, -1).astype(F32)


def _chunk_taps(w):
    k, c = w.shape
    return w.astype(F32).reshape(k, c // LANES, LANES).transpose(1, 0, 2)


def _chunk_bias(b):
    return b.astype(F32).reshape(-1, 1, LANES)


def _pad_lanes(v, width=LANES):
    return jnp.pad(v, ((0, 0), (0, width - v.shape[1])))


def _mixer_branches(x2d, seq, lam_init, mix_norm_w, w_in, q_norm_w, k_norm_w, lq1, lk1, lq2, lk2,
                    attn_subln_w, conv_w, conv_b, dt_bias, a_log, d_skip, ssm_norm_w):
    m = x2d.shape[0]
    bsz = m // seq
    dt_col = 3 * ATT_WIDTH + SSM_WIDTH + XBC_WIDTH
    wdt = _pad_lanes(w_in[:, dt_col:].astype(BF16))
    reps = ATT_WIDTH // ATT_HEAD_DIM
    q, k, v, zs, xbc, dt = _inproj(
        x2d, _row(mix_norm_w), w_in.astype(BF16), wdt,
        _row(jnp.tile(q_norm_w, reps)), _row(jnp.tile(k_norm_w, reps)),
        _chunk_taps(conv_w), _chunk_bias(conv_b), _pad_lanes(_row(dt_bias)), seq=seq)
    shp = (bsz, seq, ATT_WIDTH)
    att = _attention(q.reshape(shp), k.reshape(shp), v.reshape(shp),
                     _row(lq1), _row(lk1), _row(lq2), _row(lk2), _row(attn_subln_w),
                     lam_init=lam_init).reshape(m, ATT_WIDTH)
    y = _ssd(xbc, dt, zs, _pad_lanes(_row(a_log)), _row(jnp.repeat(d_skip, SSM_HEAD_DIM)),
             _row(ssm_norm_w), seq=seq)
    return att, y


def _ffn_layer(x2d, seq, layer, norm_w, up_all, conv_w, conv_b, down_all, mix=None):
    return _ffn(x2d, _row(norm_w), up_all, _chunk_taps(conv_w), _chunk_bias(conv_b),
                down_all, seq=seq, layer=layer, mix=mix)


def _conformer_layer(x2d, seq, norm_w, pw1_w, pw1_b, dw_w, dw_b, ln_w, ln_b, pw2_w, pw2_b):
    return _conformer(x2d, _row(norm_w), pw1_w.astype(BF16), _row(pw1_b),
                      _chunk_taps(dw_w), _chunk_bias(dw_b),
                      _row(ln_w), _row(ln_b), pw2_w.astype(BF16), _row(pw2_b), seq=seq)


def kernel(x, mix_norm_w, w_in, q_norm_w, k_norm_w, lambda_q1, lambda_k1, lambda_q2, lambda_k2,
           attn_subln_w, ssm_conv_w, ssm_conv_b, ssm_dt_bias, ssm_A_log, ssm_D, ssm_norm_w, w_out,
           conf_norm_w, conf_pw1_w, conf_pw1_b, conf_dw_w, conf_dw_b, conf_ln_w, conf_ln_b,
           conf_pw2_w, conf_pw2_b, ffn_norm_w, ffn_up_w, ffn_conv_w, ffn_conv_b, ffn_down_w):
    bsz, seq, d = x.shape
    depth = ffn_norm_w.shape[0]
    up_all = ffn_up_w.astype(BF16)
    down_all = ffn_down_w.astype(BF16)
    h = x.reshape(bsz * seq, d)
    for i in range(depth):
        mix = None
        if i % 2 == 0:
            e = i // 2
            lam_init = 0.8 - 0.6 * math.exp(-0.3 * i)
            att, y = _mixer_branches(
                h, seq, lam_init, mix_norm_w[e], w_in[e], q_norm_w[e], k_norm_w[e],
                lambda_q1[e], lambda_k1[e], lambda_q2[e], lambda_k2[e], attn_subln_w[e],
                ssm_conv_w[e], ssm_conv_b[e], ssm_dt_bias[e], ssm_A_log[e], ssm_D[e],
                ssm_norm_w[e])
            mix = (att, y, w_out[e].astype(BF16))
        else:
            o = i // 2
            h = _conformer_layer(h, seq, conf_norm_w[o], conf_pw1_w[o], conf_pw1_b[o],
                                 conf_dw_w[o], conf_dw_b[o], conf_ln_w[o], conf_ln_b[o],
                                 conf_pw2_w[o], conf_pw2_b[o])
        h = _ffn_layer(h, seq, i, ffn_norm_w[i], up_all, ffn_conv_w[i], ffn_conv_b[i],
                       down_all, mix=mix)
    return h.reshape(bsz, seq, d)
```
